```python
import jax, jax.numpy as jnp
from jax import lax
import numpy as np

D_MODEL = 1024
BATCH = 8
SEQ = 4096
DEPTH = 2

HEAD_DIM = 64
HG_HEADS = 4
HG_WIDTH = HG_HEADS * HEAD_DIM
HG_CHUNK = 64
ATT_HEADS = 8
ATT_KV_HEADS = 2
ATT_WIDTH = ATT_HEADS * HEAD_DIM
ATT_KV_WIDTH = ATT_KV_HEADS * HEAD_DIM
WINDOW = 128
ROPE_THETA = 10000.0
SG_GROUPS = 4
SG_WIDTH = SG_GROUPS * HEAD_DIM
SG_CHUNK = 128

MIX_WIDTH = HG_WIDTH + ATT_WIDTH + SG_WIDTH
IN_SPLITS = (HG_WIDTH, HG_WIDTH, HG_WIDTH, HG_WIDTH,
             ATT_WIDTH, ATT_KV_WIDTH, ATT_KV_WIDTH,
             SG_WIDTH, SG_WIDTH)
IN_WIDTH = sum(IN_SPLITS)
D_FF = 2816
PLE_DIM = 256
N_NORMS = 8
EPS = 1e-6
MASK_VALUE = -1e30
LB_FLOOR = 1e-30

kernel_name = "hybrid_hgrn2_swa_sink_sgu_macaron"


def rmsnorm(x, g):
    xf = x.astype(jnp.float32)
    y = xf * lax.rsqrt(jnp.mean(xf * xf, axis=-1, keepdims=True) + EPS)
    return (y * g.astype(jnp.float32)).astype(x.dtype)


def layernorm(x, g):
    xf = x.astype(jnp.float32)
    mu = jnp.mean(xf, axis=-1, keepdims=True)
    xc = xf - mu
    y = xc * lax.rsqrt(jnp.mean(xc * xc, axis=-1, keepdims=True) + EPS)
    return (y * g.astype(jnp.float32)).astype(x.dtype)


def swiglu(x, w_gu, w_down):
    gate, up = jnp.split(x @ w_gu, 2, axis=-1)
    return (jax.nn.silu(gate) * up) @ w_down


def rope(x, pos):
    half = x.shape[-1] // 2
    inv = ROPE_THETA ** (-jnp.arange(half, dtype=jnp.float32) / half)
    ang = pos.astype(jnp.float32)[..., None] * inv
    cos = jnp.cos(ang)[:, :, None, :]
    sin = jnp.sin(ang)[:, :, None, :]
    xf = x.astype(jnp.float32)
    x1, x2 = xf[..., :half], xf[..., half:]
    return jnp.concatenate([x1 * cos - x2 * sin, x2 * cos + x1 * sin], axis=-1).astype(x.dtype)


def hgrn2(q, f_raw, i, lb):
    B, S, _ = q.shape
    C = HG_CHUNK
    n = S // C
    q = q.astype(jnp.float32)
    v = i.astype(jnp.float32)
    lb = lb.astype(jnp.float32)
    logf = jnp.logaddexp(jnp.log(jnp.maximum(lb, LB_FLOOR)),
                         jnp.log1p(-lb) + jax.nn.log_sigmoid(f_raw.astype(jnp.float32)))
    k = -jnp.expm1(logf)

    def heads(t):
        return t.reshape(B, n, C, HG_HEADS, HEAD_DIM).transpose(1, 0, 3, 2, 4)

    causal = jnp.tril(jnp.ones((C, C), dtype=bool))[:, :, None]

    def step(state, inp):
        qc, kc, vc, lc = inp
        b = jnp.cumsum(lc, axis=2)
        o_inter = jnp.einsum('bhcd,bhde->bhce', qc * jnp.exp(b), state)
        diff = b[:, :, :, None, :] - b[:, :, None, :, :]
        dec = jnp.where(causal, jnp.exp(jnp.where(causal, diff, 0.0)), 0.0)
        att = jnp.einsum('bhid,bhjd,bhijd->bhij', qc, kc, dec)
        o_intra = jnp.einsum('bhij,bhje->bhie', att, vc)
        b_last = b[:, :, -1:, :]
        new_state = (jnp.exp(b_last[:, :, 0, :])[..., None] * state
                     + jnp.einsum('bhcd,bhce->bhde', kc * jnp.exp(b_last - b), vc))
        return new_state, o_inter + o_intra

    s0 = jnp.zeros((B, HG_HEADS, HEAD_DIM, HEAD_DIM), jnp.float32)
    _, o = lax.scan(step, s0, (heads(q), heads(k), heads(v), heads(logf)))
    return o.transpose(1, 0, 3, 2, 4).reshape(B, S, HG_HEADS, HEAD_DIM)


def swa_sink_attention(q, k, v, sinks, pos):
    B, S, _ = q.shape
    G = ATT_HEADS // ATT_KV_HEADS
    nb = S // WINDOW
    q = rope(q.reshape(B, S, ATT_HEADS, HEAD_DIM), pos)
    k = rope(k.reshape(B, S, ATT_KV_HEADS, HEAD_DIM), pos)
    v = v.reshape(B, S, ATT_KV_HEADS, HEAD_DIM)
    qb = q.reshape(B, nb, WINDOW, ATT_KV_HEADS, G, HEAD_DIM)

    def with_prev(t):
        t = t.reshape(B, nb, WINDOW, ATT_KV_HEADS, HEAD_DIM)
        prev = jnp.concatenate([jnp.zeros_like(t[:, :1]), t[:, :-1]], axis=1)
        return jnp.concatenate([prev, t], axis=2)

    kk, vv = with_prev(k), with_prev(v)
    s = jnp.einsum('bnqkgd,bnskd->bnkgqs', qb, kk).astype(jnp.float32) * (HEAD_DIM ** -0.5)
    qi = jnp.arange(WINDOW)[:, None]
    sj = jnp.arange(2 * WINDOW)[None, :]
    rel = qi + WINDOW - sj
    band = (rel >= 0) & (rel < WINDOW)
    exists = (jnp.arange(nb)[:, None, None] * WINDOW + sj[None] - WINDOW) >= 0
    mask = (band[None] & exists)[None, :, None, None]
    s = jnp.where(mask, s, MASK_VALUE)
    sink = sinks.astype(jnp.float32).reshape(1, 1, ATT_KV_HEADS, G, 1, 1)
    m = jnp.maximum(jnp.max(s, axis=-1, keepdims=True), sink)
    pr = jnp.where(mask, jnp.exp(s - m), 0.0)
    denom = jnp.sum(pr, axis=-1, keepdims=True) + jnp.exp(sink - m)
    o = jnp.einsum('bnkgqs,bnskd->bnqkgd', (pr / denom).astype(v.dtype), vv)
    return o.reshape(B, S, ATT_WIDTH)


def spatial_gating(u, v, ln_g, w_s, b_s):
    B, S, _ = u.shape
    nc = S // SG_CHUNK
    u = jax.nn.gelu(u)
    v = layernorm(jax.nn.gelu(v), ln_g)
    vg = v.reshape(B, nc, SG_CHUNK, SG_GROUPS, HEAD_DIM)
    w = w_s * jnp.tril(jnp.ones((SG_CHUNK, SG_CHUNK), w_s.dtype))
    mix = jnp.einsum('gts,bnsgd->bntgd', w, vg) + b_s.T[:, :, None]
    return (u.reshape(B, nc, SG_CHUNK, SG_GROUPS, HEAD_DIM) * mix).reshape(B, S, SG_WIDTH)


def hybrid_layer(h, p_i, pos, lb, norm_g, w_in, w_out, ffn1_gu, ffn1_down, ffn2_gu, ffn2_down,
                 hg_norm_g, sinks, sg_ln_g, sg_w, sg_b, ple_proj, ple_gate):
    B, S, _ = h.shape
    h = h + 0.5 * rmsnorm(swiglu(rmsnorm(h, norm_g[0]), ffn1_gu, ffn1_down), norm_g[1])
    z = rmsnorm(h, norm_g[2]) @ w_in
    idx = [int(c) for c in np.cumsum(IN_SPLITS)[:-1]]
    hq, hf, hi, hg, aq, ak, av, su, sv = jnp.split(z, idx, axis=-1)
    o_a = hgrn2(hq, hf, hi, lb)
    o_a = rmsnorm(o_a, hg_norm_g) * jax.nn.sigmoid(hg.astype(jnp.float32).reshape(B, S, HG_HEADS, HEAD_DIM))
    o_a = o_a.reshape(B, S, HG_WIDTH).astype(h.dtype)
    o_b = swa_sink_attention(aq, ak, av, sinks, pos).astype(h.dtype)
    o_c = spatial_gating(su, sv, sg_ln_g, sg_w, sg_b).astype(h.dtype)
    mixed = jnp.concatenate([o_a, o_b, o_c], axis=-1) @ w_out
    h = h + rmsnorm(mixed, norm_g[3])
    h = h + 0.5 * rmsnorm(swiglu(rmsnorm(h, norm_g[4]), ffn2_gu, ffn2_down), norm_g[5])
    gate = jax.nn.sigmoid(rmsnorm(h, norm_g[6]) @ ple_gate)
    h = h + rmsnorm((p_i @ ple_proj) * gate, norm_g[7])
    return h


def setup_inputs(seed: int = 0) -> dict:
    key = jax.random.key(seed)
    ks = jax.random.split(key, 20)

    def nrm(k, shape, scale):
        return jax.random.normal(k, shape, jnp.float32) * scale

    x = nrm(ks[0], (BATCH, SEQ, D_MODEL), 1.0)
    p = nrm(ks[1], (DEPTH, BATCH, SEQ, PLE_DIM), 1.0)
    positions = (jnp.arange(SEQ, dtype=jnp.int32)[None, :]
                 + jax.random.randint(ks[2], (BATCH, 1), 0, 1024, dtype=jnp.int32))
    return {
        "x": x,
        "p": p,
        "positions": positions,
        "norm_gains": 1.0 + nrm(ks[3], (DEPTH, N_NORMS, D_MODEL), 0.05),
        "w_in": nrm(ks[4], (DEPTH, D_MODEL, IN_WIDTH), D_MODEL ** -0.5),
        "w_out": nrm(ks[5], (DEPTH, MIX_WIDTH, D_MODEL), MIX_WIDTH ** -0.5),
        "ffn1_gate_up": nrm(ks[6], (DEPTH, D_MODEL, 2 * D_FF), D_MODEL ** -0.5),
        "ffn1_down": nrm(ks[7], (DEPTH, D_FF, D_MODEL), D_FF ** -0.5),
        "ffn2_gate_up": nrm(ks[8], (DEPTH, D_MODEL, 2 * D_FF), D_MODEL ** -0.5),
        "ffn2_down": nrm(ks[9], (DEPTH, D_FF, D_MODEL), D_FF ** -0.5),
        "hgrn_lb_logits": nrm(ks[10], (DEPTH, HG_WIDTH), 0.5),
        "hgrn_norm_gain": 1.0 + nrm(ks[11], (DEPTH, HEAD_DIM), 0.05),
        "attn_sinks": nrm(ks[12], (DEPTH, ATT_HEADS), 0.5),
        "sg_ln_gain": 1.0 + nrm(ks[13], (DEPTH, SG_WIDTH), 0.05),
        "sg_spatial_w": nrm(ks[14], (DEPTH, SG_GROUPS, SG_CHUNK, SG_CHUNK), SG_CHUNK ** -0.5),
        "sg_spatial_b": 1.0 + nrm(ks[15], (DEPTH, SG_GROUPS, SG_CHUNK), 0.1),
        "ple_proj": nrm(ks[16], (DEPTH, PLE_DIM, D_MODEL), PLE_DIM ** -0.5),
        "ple_gate": nrm(ks[17], (DEPTH, D_MODEL, D_MODEL), D_MODEL ** -0.5),
    }


def reference(x, p, positions, norm_gains, w_in, w_out, ffn1_gate_up, ffn1_down, ffn2_gate_up,
              ffn2_down, hgrn_lb_logits, hgrn_norm_gain, attn_sinks, sg_ln_gain, sg_spatial_w,
              sg_spatial_b, ple_proj, ple_gate):
    probs = jax.nn.softmax(hgrn_lb_logits.astype(jnp.float32), axis=0)
    lower_bounds = jnp.cumsum(probs, axis=0) - probs[0]
    h = x
    for l in range(DEPTH):
        h = hybrid_layer(h, p[l], positions, lower_bounds[l], norm_gains[l], w_in[l], w_out[l],
                         ffn1_gate_up[l], ffn1_down[l], ffn2_gate_up[l], ffn2_down[l],
                         hgrn_norm_gain[l], attn_sinks[l], sg_ln_gain[l], sg_spatial_w[l],
                         sg_spatial_b[l], ple_proj[l], ple_gate[l])
    return h
```

```python
import functools

import numpy as np
import jax
import jax.numpy as jnp
from jax import lax
from jax.experimental import pallas as pl
from jax.experimental.pallas import tpu as pltpu

F32 = jnp.float32
BF16 = jnp.bfloat16

D_MODEL = 1024
HEAD_DIM = 64
HG_HEADS = 4
HG_WIDTH = HG_HEADS * HEAD_DIM
ATT_HEADS = 8
ATT_KV_HEADS = 2
ATT_WIDTH = ATT_HEADS * HEAD_DIM
ATT_KV_WIDTH = ATT_KV_HEADS * HEAD_DIM
WINDOW = 128
ROPE_THETA = 10000.0
SG_GROUPS = 4
SG_WIDTH = SG_GROUPS * HEAD_DIM
SG_CHUNK = 128
IN_WIDTH = 4 * HG_WIDTH + ATT_WIDTH + 2 * ATT_KV_WIDTH + 2 * SG_WIDTH
D_FF = 2816
PLE_DIM = 256
EPS = 1e-6
MASK_VALUE = -1e30
LB_FLOOR = 1e-30

COL_HG = 0
COL_AQ = 4 * HG_WIDTH
COL_AK = COL_AQ + ATT_WIDTH
COL_AV = COL_AK + ATT_KV_WIDTH
COL_SU = COL_AV + ATT_KV_WIDTH
COL_SV = COL_SU + SG_WIDTH

LANES = 128
TOKEN_TILE = 512
FF_CHUNK = 256
HG_TILE = 256
HG_SUB = 16
VMEM_LIMIT = 48 * 1024 * 1024


def _const_spec(shape):
    nd = len(shape)
    return pl.BlockSpec(shape, lambda *_: (0,) * nd, pipeline_mode=pl.Buffered(1))


def _rms(x, g):
    ms = jnp.mean(x * x, axis=-1, keepdims=True)
    return x * lax.rsqrt(ms + EPS) * g


def _dot(a, b):
    return jnp.dot(a, b, preferred_element_type=F32)


def _dot_nt(a, b):
    return lax.dot_general(a, b, (((1,), (1,)), ((), ())), preferred_element_type=F32)


def _params(n_axes, semantics="parallel"):
    return pltpu.CompilerParams(dimension_semantics=(semantics,) * n_axes, vmem_limit_bytes=VMEM_LIMIT)


def _ffn_kernel(h_ref, gpre_ref, gpost_ref, wg_ref, wu_ref, wd_ref, o_ref, act_ref):
    h = h_ref[...]
    x = _rms(h, gpre_ref[...]).astype(BF16)
    for c in range(D_FF // FF_CHUNK):
        g = _dot(x, wg_ref[c])
        u = _dot(x, wu_ref[c])
        act_ref[:, c * FF_CHUNK:(c + 1) * FF_CHUNK] = (g * jax.nn.sigmoid(g) * u).astype(BF16)
    y = _dot(act_ref[...], wd_ref[...])
    o_ref[...] = h + 0.5 * _rms(y, gpost_ref[...])


def _ffn(h, g_pre, g_post, wg, wu, wd):
    n = h.shape[0]
    tm = min(TOKEN_TILE, n)
    row = pl.BlockSpec((tm, D_MODEL), lambda i: (i, 0))
    return pl.pallas_call(
        _ffn_kernel,
        out_shape=jax.ShapeDtypeStruct((n, D_MODEL), F32),
        grid=(n // tm,),
        in_specs=[row, _const_spec((1, D_MODEL)), _const_spec((1, D_MODEL)),
                  _const_spec(wg.shape), _const_spec(wu.shape), _const_spec(wd.shape)],
        out_specs=row,
        scratch_shapes=[pltpu.VMEM((tm, D_FF), BF16)],
        compiler_params=_params(1),
        name="ffn",
    )(h, g_pre, g_post, wg, wu, wd)


def _inproj_kernel(h_ref, g_ref, w_ref, z_ref):
    x = _rms(h_ref[...], g_ref[...]).astype(BF16)
    z_ref[...] = _dot(x, w_ref[...])


def _inproj(h, g, w_in):
    n = h.shape[0]
    tm = min(TOKEN_TILE, n)
    return pl.pallas_call(
        _inproj_kernel,
        out_shape=jax.ShapeDtypeStruct((n, IN_WIDTH), F32),
        grid=(n // tm,),
        in_specs=[pl.BlockSpec((tm, D_MODEL), lambda i: (i, 0)), _const_spec((1, D_MODEL)),
                  _const_spec(w_in.shape)],
        out_specs=pl.BlockSpec((tm, IN_WIDTH), lambda i: (i, 0)),
        compiler_params=_params(1),
        name="inproj",
    )(h, g, w_in)


def _outproj_kernel(h_ref, oa_ref, ob_ref, oc_ref, g_ref, w_ref, o_ref):
    y = _dot(oa_ref[...].astype(BF16), w_ref[0:HG_WIDTH, :])
    y = y + _dot(ob_ref[...].astype(BF16), w_ref[HG_WIDTH:HG_WIDTH + ATT_WIDTH, :])
    y = y + _dot(oc_ref[...].astype(BF16), w_ref[HG_WIDTH + ATT_WIDTH:, :])
    o_ref[...] = h_ref[...] + _rms(y, g_ref[...])


def _outproj(h, o_a, o_b, o_c, g, w_out):
    n = h.shape[0]
    tm = min(TOKEN_TILE, n)
    row = lambda w: pl.BlockSpec((tm, w), lambda i: (i, 0))
    return pl.pallas_call(
        _outproj_kernel,
        out_shape=jax.ShapeDtypeStruct((n, D_MODEL), F32),
        grid=(n // tm,),
        in_specs=[row(D_MODEL), row(HG_WIDTH), row(ATT_WIDTH), row(SG_WIDTH), _const_spec((1, D_MODEL)),
                  _const_spec(w_out.shape)],
        out_specs=row(D_MODEL),
        compiler_params=_params(1),
        name="outproj",
    )(h, o_a, o_b, o_c, g, w_out)


def _ple_kernel(h_ref, p_ref, gin_ref, gout_ref, wgate_ref, wproj_ref, o_ref):
    h = h_ref[...]
    gate = jax.nn.sigmoid(_dot(_rms(h, gin_ref[...]).astype(BF16), wgate_ref[...]))
    e = _dot(p_ref[...].astype(BF16), wproj_ref[...])
    o_ref[...] = h + _rms(e * gate, gout_ref[...])


def _ple(h, p, g_in, g_out, w_gate, w_proj):
    n = h.shape[0]
    tm = min(TOKEN_TILE, n)
    row = lambda w: pl.BlockSpec((tm, w), lambda i: (i, 0))
    return pl.pallas_call(
        _ple_kernel,
        out_shape=jax.ShapeDtypeStruct((n, D_MODEL), F32),
        grid=(n // tm,),
        in_specs=[row(D_MODEL), row(PLE_DIM), _const_spec((1, D_MODEL)), _const_spec((1, D_MODEL)),
                  _const_spec(w_gate.shape), _const_spec(w_proj.shape)],
        out_specs=row(D_MODEL),
        compiler_params=_params(1),
        name="ple",
    )(h, p, g_in, g_out, w_gate, w_proj)


def _rope_table_kernel(pos_ref, inv_ref, sign_ref, cos_ref, sin_ref):
    ang = pos_ref[...] * inv_ref[...]
    cos_ref[...] = jnp.cos(ang)
    sin_ref[...] = jnp.sin(ang) * sign_ref[...]


def _rope_tables(positions):
    n = positions.size
    half = HEAD_DIM // 2
    inv = ROPE_THETA ** (-jnp.arange(half, dtype=F32) / half)
    inv = jnp.tile(inv, LANES // half)[None, :]
    sign = jnp.asarray(np.where((np.arange(LANES) % HEAD_DIM) < half, -1.0, 1.0)[None, :], F32)
    pos = jnp.broadcast_to(positions.astype(F32).reshape(n, 1), (n, LANES))
    tm = min(TOKEN_TILE, n)
    row = pl.BlockSpec((tm, LANES), lambda i: (i, 0))
    return pl.pallas_call(
        _rope_table_kernel,
        out_shape=[jax.ShapeDtypeStruct((n, LANES), F32)] * 2,
        grid=(n // tm,),
        in_specs=[row, _const_spec((1, LANES)), _const_spec((1, LANES))],
        out_specs=[row, row],
        compiler_params=_params(1),
        name="rope_tables",
    )(pos, inv, sign)


def _swap_halves(x):
    w = x.shape[-1]
    half = HEAD_DIM // 2
    lane = lax.broadcasted_iota(jnp.int32, x.shape, 1)
    first = (lane % HEAD_DIM) < half
    return jnp.where(first, pltpu.roll(x, w - half, 1), pltpu.roll(x, half, 1))


def _attn_kernel(sink_ref, q_ref, kc_ref, kp_ref, vc_ref, vp_ref, cc_ref, cp_ref, sc_ref, sp_ref, o_ref):
    n = pl.program_id(1)
    cos_c, sin_c = cc_ref[...], sc_ref[...]
    q = q_ref[...]
    reps = ATT_WIDTH // LANES
    q = q * jnp.concatenate([cos_c] * reps, axis=1) + _swap_halves(q) * jnp.concatenate([sin_c] * reps, axis=1)
    kc = kc_ref[...]
    kc = kc * cos_c + _swap_halves(kc) * sin_c
    kp = kp_ref[...]
    kp = kp * cp_ref[...] + _swap_halves(kp) * sp_ref[...]
    k = jnp.concatenate([kp, kc], axis=0).astype(BF16)
    v = jnp.concatenate([vp_ref[...], vc_ref[...]], axis=0).astype(BF16)
    q = q.astype(BF16)

    qi = lax.broadcasted_iota(jnp.int32, (WINDOW, 2 * WINDOW), 0)
    sj = lax.broadcasted_iota(jnp.int32, (WINDOW, 2 * WINDOW), 1)
    rel = qi + WINDOW - sj
    mask = (rel >= 0) & (rel < WINDOW) & ((sj >= WINDOW) | (n > 0))

    group = ATT_HEADS // ATT_KV_HEADS
    outs = []
    for hd in range(ATT_HEADS):
        kv = hd // group
        qh = q[:, hd * HEAD_DIM:(hd + 1) * HEAD_DIM]
        kh = k[:, kv * HEAD_DIM:(kv + 1) * HEAD_DIM]
        vh = v[:, kv * HEAD_DIM:(kv + 1) * HEAD_DIM]
        s = _dot_nt(qh, kh) * (HEAD_DIM ** -0.5)
        s = jnp.where(mask, s, MASK_VALUE)
        sink = sink_ref[hd]
        m = jnp.maximum(jnp.max(s, axis=-1, keepdims=True), sink)
        pr = jnp.where(mask, jnp.exp(s - m), 0.0)
        denom = jnp.sum(pr, axis=-1, keepdims=True) + jnp.exp(sink - m)
        outs.append(_dot((pr / denom).astype(BF16), vh))
    o_ref[...] = jnp.concatenate(outs, axis=1)


def _attention(z, sinks, cos, sin, batch, seq):
    nb = seq // WINDOW
    cur = lambda b, n: b * nb + n
    prev = lambda b, n: b * nb + jnp.maximum(n - 1, 0)
    kcol, vcol = COL_AK // ATT_KV_WIDTH, COL_AV // ATT_KV_WIDTH
    blk = lambda w, rowf, col: pl.BlockSpec((WINDOW, w), lambda b, n: (rowf(b, n), col))
    return pl.pallas_call(
        _attn_kernel,
        out_shape=jax.ShapeDtypeStruct((batch * seq, ATT_WIDTH), F32),
        grid=(batch, nb),
        in_specs=[pl.BlockSpec(memory_space=pltpu.SMEM),
                  blk(ATT_WIDTH, cur, COL_AQ // ATT_WIDTH),
                  blk(ATT_KV_WIDTH, cur, kcol), blk(ATT_KV_WIDTH, prev, kcol),
                  blk(ATT_KV_WIDTH, cur, vcol), blk(ATT_KV_WIDTH, prev, vcol),
                  blk(LANES, cur, 0), blk(LANES, prev, 0), blk(LANES, cur, 0), blk(LANES, prev, 0)],
        out_specs=pl.BlockSpec((WINDOW, ATT_WIDTH), lambda b, n: (cur(b, n), 0)),
        compiler_params=_params(2),
        name="swa_attention",
    )(sinks, z, z, z, z, z, cos, cos, sin, sin)


def _sgu_kernel(u_ref, v_ref, lng_ref, w_ref, bias_ref, o_ref):
    u = jax.nn.gelu(u_ref[...])
    v = jax.nn.gelu(v_ref[...])
    mu = jnp.mean(v, axis=-1, keepdims=True)
    vc = v - mu
    v = (vc * lax.rsqrt(jnp.mean(vc * vc, axis=-1, keepdims=True) + EPS) * lng_ref[...]).astype(BF16)
    ti = lax.broadcasted_iota(jnp.int32, (SG_CHUNK, SG_CHUNK), 0)
    si = lax.broadcasted_iota(jnp.int32, (SG_CHUNK, SG_CHUNK), 1)
    causal = si <= ti
    ws = [jnp.where(causal, w_ref[g], 0.0).astype(BF16) for g in range(SG_GROUPS)]
    lane_group = lax.broadcasted_iota(jnp.int32, (SG_CHUNK, SG_WIDTH), 1) // HEAD_DIM
    bias = bias_ref[...]
    for c in range(u.shape[0] // SG_CHUNK):
        rows = slice(c * SG_CHUNK, (c + 1) * SG_CHUNK)
        vch = v[rows]
        mix = bias
        for g in range(SG_GROUPS):
            mix = mix + jnp.where(lane_group == g, _dot(ws[g], vch), 0.0)
        o_ref[rows, :] = u[rows] * mix


def _sgu(z, ln_g, w_s, bias_tab):
    n = z.shape[0]
    tm = min(TOKEN_TILE, n)
    return pl.pallas_call(
        _sgu_kernel,
        out_shape=jax.ShapeDtypeStruct((n, SG_WIDTH), F32),
        grid=(n // tm,),
        in_specs=[pl.BlockSpec((tm, SG_WIDTH), lambda i: (i, COL_SU // SG_WIDTH)),
                  pl.BlockSpec((tm, SG_WIDTH), lambda i: (i, COL_SV // SG_WIDTH)),
                  _const_spec((1, SG_WIDTH)), _const_spec(w_s.shape), _const_spec(bias_tab.shape)],
        out_specs=pl.BlockSpec((tm, SG_WIDTH), lambda i: (i, 0)),
        compiler_params=_params(1),
        name="spatial_gating",
    )(z, z, ln_g, w_s, bias_tab)


def _group_cumsum(x):
    n = x.shape[0]
    r = lax.broadcasted_iota(jnp.int32, x.shape, 0) % HG_SUB
    s = 1
    while s < HG_SUB:
        x = x + jnp.where(r >= s, pltpu.roll(x, s, 0), 0.0)
        s *= 2
    return x


def _group_last(x):
    n = x.shape[0]
    r = lax.broadcasted_iota(jnp.int32, x.shape, 0) % HG_SUB
    t = jnp.where(r == HG_SUB - 1, x, 0.0)
    s = 1
    while s < HG_SUB:
        t = t + pltpu.roll(t, n - s, 0)
        s *= 2
    return t


def _head_block_mask(shape):
    r = lax.broadcasted_iota(jnp.int32, shape, 0) // HEAD_DIM
    c = lax.broadcasted_iota(jnp.int32, shape, 1) // HEAD_DIM
    return r == c


def _hgrn_kernel(z_ref, lbl_ref, gn_ref, o_ref, st_ref, b_ref, k_ref, qd_ref, kd_ref, vt_ref, raw_ref, *, layer):
    @pl.when(pl.program_id(1) == 0)
    def _():
        st_ref[...] = jnp.zeros_like(st_ref)

    w = HG_WIDTH
    q = z_ref[:, 0:w]
    f_raw = z_ref[:, w:2 * w]
    v = z_ref[:, 2 * w:3 * w]

    lg = lbl_ref[...]
    e = jnp.exp(lg - jnp.max(lg, axis=0, keepdims=True))
    probs = e / jnp.sum(e, axis=0, keepdims=True)
    lb = jnp.sum(probs[0:layer + 1], axis=0, keepdims=True) - probs[0:1]

    log_sig = jnp.minimum(f_raw, 0.0) - jnp.log1p(jnp.exp(-jnp.abs(f_raw)))
    a = jnp.log(jnp.maximum(lb, LB_FLOOR))
    c = jnp.log1p(-lb) + log_sig
    logf = jnp.maximum(a, c) + jnp.log1p(jnp.exp(-jnp.abs(a - c)))
    k = 1.0 - jnp.exp(logf)

    b = _group_cumsum(logf)
    b_last = _group_last(b)
    b_ref[...] = b
    k_ref[...] = k
    qd_ref[...] = (q * jnp.exp(b)).astype(BF16)
    kd_ref[...] = (k * jnp.exp(b_last - b)).astype(BF16)
    vt_ref[...] = v.T.astype(BF16)

    blockdiag = _head_block_mask((w, w))
    ones_bd = jnp.where(blockdiag, 1.0, 0.0).astype(BF16)
    tok_group = lax.broadcasted_iota(jnp.int32, (w, HG_TILE), 1) // HG_SUB
    row = lax.broadcasted_iota(jnp.int32, (HG_SUB, w), 0)

    def step(m, carry):
        sl = pl.ds(pl.multiple_of(m * HG_SUB, HG_SUB), HG_SUB)
        st = st_ref[...]
        o_inter = _dot_nt(qd_ref[sl, :], st.astype(BF16))

        bm = b_ref[sl, :]
        qm = z_ref[sl, 0:w]
        km = k_ref[sl, :]
        vm = z_ref[sl, 2 * w:3 * w]
        pieces = []
        for j in range(HG_SUB):
            diff = bm - bm[j:j + 1, :]
            causal = row >= j
            dec = jnp.where(causal, jnp.exp(jnp.where(causal, diff, 0.0)), 0.0)
            pieces.append((qm * km[j:j + 1, :] * dec).astype(BF16))
        att = _dot(jnp.concatenate(pieces, axis=0), ones_bd)
        o_intra = att[0:HG_SUB] * vm[0:1, :]
        for j in range(1, HG_SUB):
            o_intra = o_intra + att[j * HG_SUB:(j + 1) * HG_SUB] * vm[j:j + 1, :]
        raw_ref[sl, :] = o_inter + o_intra

        vt_m = jnp.where(tok_group == m, vt_ref[...], jnp.zeros((), BF16))
        upd = _dot(vt_m, kd_ref[...])
        st_ref[...] = st * jnp.exp(bm[HG_SUB - 1:HG_SUB, :]) + jnp.where(blockdiag, upd, 0.0)
        return carry

    lax.fori_loop(0, HG_TILE // HG_SUB, step, 0)

    o = raw_ref[...]
    sq = o * o
    hi = sq.astype(BF16)
    lo = (sq - hi.astype(F32)).astype(BF16)
    ms = (_dot(hi, ones_bd) + _dot(lo, ones_bd)) * (1.0 / HEAD_DIM)
    gate = jax.nn.sigmoid(z_ref[:, 3 * w:4 * w])
    o_ref[...] = o * lax.rsqrt(ms + EPS) * gn_ref[...] * gate


def _hgrn(z, lb_logits, norm_gain, layer, batch, seq):
    nt = seq // HG_TILE
    w = HG_WIDTH
    return pl.pallas_call(
        functools.partial(_hgrn_kernel, layer=layer),
        out_shape=jax.ShapeDtypeStruct((batch * seq, w), F32),
        grid=(batch, nt),
        in_specs=[pl.BlockSpec((HG_TILE, 4 * w), lambda b, t: (b * nt + t, COL_HG)),
                  _const_spec(lb_logits.shape), _const_spec((1, w))],
        out_specs=pl.BlockSpec((HG_TILE, w), lambda b, t: (b * nt + t, 0)),
        scratch_shapes=[pltpu.VMEM((w, w), F32),
                        pltpu.VMEM((HG_TILE, w), F32),
                        pltpu.VMEM((HG_TILE, w), F32),
                        pltpu.VMEM((HG_TILE, w), BF16),
                        pltpu.VMEM((HG_TILE, w), BF16),
                        pltpu.VMEM((w, HG_TILE), BF16),
                        pltpu.VMEM((HG_TILE, w), F32)],
        compiler_params=_params(2, "arbitrary"),
        name="hgrn2",
    )(z, lb_logits, norm_gain)


def kernel(x, p, positions, norm_gains, w_in, w_out, ffn1_gate_up, ffn1_down, ffn2_gate_up, ffn2_down,
           hgrn_lb_logits, hgrn_norm_gain, attn_sinks, sg_ln_gain, sg_spatial_w, sg_spatial_b, ple_proj, ple_gate):
    batch, seq, _ = x.shape
    depth = norm_gains.shape[0]
    n = batch * seq
    h = x.reshape(n, D_MODEL)
    cos, sin = _rope_tables(positions)

    def split_gate_up(w):
        chunks = D_FF // FF_CHUNK
        to_chunks = lambda m: m.reshape(D_MODEL, chunks, FF_CHUNK).transpose(1, 0, 2).astype(BF16)
        return to_chunks(w[:, :D_FF]), to_chunks(w[:, D_FF:])

    for l in range(depth):
        gains = norm_gains[l].reshape(norm_gains.shape[1], 1, D_MODEL)
        wg1, wu1 = split_gate_up(ffn1_gate_up[l])
        wg2, wu2 = split_gate_up(ffn2_gate_up[l])
        h = _ffn(h, gains[0], gains[1], wg1, wu1, ffn1_down[l].astype(BF16))
        z = _inproj(h, gains[2], w_in[l].astype(BF16))
        o_a = _hgrn(z, hgrn_lb_logits, jnp.tile(hgrn_norm_gain[l], HG_HEADS)[None, :], l, batch, seq)
        o_b = _attention(z, attn_sinks[l], cos, sin, batch, seq)
        bias_tab = jnp.repeat(sg_spatial_b[l].T, HEAD_DIM, axis=1)
        o_c = _sgu(z, sg_ln_gain[l][None, :], sg_spatial_w[l], bias_tab)
        h = _outproj(h, o_a, o_b, o_c, gains[3], w_out[l].astype(BF16))
        h = _ffn(h, gains[4], gains[5], wg2, wu2, ffn2_down[l].astype(BF16))
        h = _ple(h, p[l].reshape(n, PLE_DIM), gains[6], gains[7], ple_gate[l].astype(BF16), ple_proj[l].astype(BF16))
    return h.reshape(batch, seq, D_MODEL)
```

```python
import functools

import numpy as np
import jax
import jax.numpy as jnp
from jax import lax
from jax.experimental import pallas as pl
from jax.experimental.pallas import tpu as pltpu

F32 = jnp.float32
BF16 = jnp.bfloat16

D_MODEL = 1024
HEAD_DIM = 64
HG_HEADS = 4
HG_WIDTH = HG_HEADS * HEAD_DIM
ATT_HEADS = 8
ATT_KV_HEADS = 2
ATT_WIDTH = ATT_HEADS * HEAD_DIM
ATT_KV_WIDTH = ATT_KV_HEADS * HEAD_DIM
WINDOW = 128
ROPE_THETA = 10000.0
SG_GROUPS = 4
SG_WIDTH = SG_GROUPS * HEAD_DIM
SG_CHUNK = 128
IN_WIDTH = 4 * HG_WIDTH + ATT_WIDTH + 2 * ATT_KV_WIDTH + 2 * SG_WIDTH
D_FF = 2816
PLE_DIM = 256
EPS = 1e-6
MASK_VALUE = -1e30
LB_FLOOR = 1e-30
LOG2E = 1.4426950408889634

COL_HG = 0
COL_AQ = 4 * HG_WIDTH
COL_AK = COL_AQ + ATT_WIDTH
COL_AV = COL_AK + ATT_KV_WIDTH
COL_SU = COL_AV + ATT_KV_WIDTH
COL_SV = COL_SU + SG_WIDTH

LANES = 128
TOKEN_TILE = 512
FF_CHUNK = 256
HG_TILE = 256
HG_SUB = 16
VMEM_LIMIT = 48 * 1024 * 1024


def _const_spec(shape):
    nd = len(shape)
    return pl.BlockSpec(shape, lambda *_: (0,) * nd, pipeline_mode=pl.Buffered(1))


def _rms(x, g):
    ms = jnp.mean(x * x, axis=-1, keepdims=True)
    return x * lax.rsqrt(ms + EPS) * g


def _dot(a, b):
    return jnp.dot(a, b, preferred_element_type=F32)


def _dot_nt(a, b):
    return lax.dot_general(a, b, (((1,), (1,)), ((), ())), preferred_element_type=F32)


def _params(n_axes, semantics="parallel"):
    return pltpu.CompilerParams(dimension_semantics=(semantics,) * n_axes, vmem_limit_bytes=VMEM_LIMIT)


def _ffn_kernel(h_ref, gpre_ref, gpost_ref, wg_ref, wu_ref, wd_ref, o_ref, act_ref):
    h = h_ref[...]
    x = _rms(h, gpre_ref[...]).astype(BF16)
    for c in range(D_FF // FF_CHUNK):
        g = _dot(x, wg_ref[c])
        u = _dot(x, wu_ref[c])
        act_ref[:, c * FF_CHUNK:(c + 1) * FF_CHUNK] = (g * jax.nn.sigmoid(g) * u).astype(BF16)
    y = _dot(act_ref[...], wd_ref[...])
    o_ref[...] = h + 0.5 * _rms(y, gpost_ref[...])


def _ffn(h, g_pre, g_post, wg, wu, wd):
    n = h.shape[0]
    tm = min(TOKEN_TILE, n)
    row = pl.BlockSpec((tm, D_MODEL), lambda i: (i, 0))
    return pl.pallas_call(
        _ffn_kernel,
        out_shape=jax.ShapeDtypeStruct((n, D_MODEL), F32),
        grid=(n // tm,),
        in_specs=[row, _const_spec((1, D_MODEL)), _const_spec((1, D_MODEL)),
                  _const_spec(wg.shape), _const_spec(wu.shape), _const_spec(wd.shape)],
        out_specs=row,
        scratch_shapes=[pltpu.VMEM((tm, D_FF), BF16)],
        compiler_params=_params(1),
        name="ffn",
    )(h, g_pre, g_post, wg, wu, wd)


def _inproj_kernel(h_ref, g_ref, w_ref, z_ref):
    x = _rms(h_ref[...], g_ref[...]).astype(BF16)
    z_ref[...] = _dot(x, w_ref[...])


def _inproj(h, g, w_in):
    n = h.shape[0]
    tm = min(TOKEN_TILE, n)
    return pl.pallas_call(
        _inproj_kernel,
        out_shape=jax.ShapeDtypeStruct((n, IN_WIDTH), F32),
        grid=(n // tm,),
        in_specs=[pl.BlockSpec((tm, D_MODEL), lambda i: (i, 0)), _const_spec((1, D_MODEL)),
                  _const_spec(w_in.shape)],
        out_specs=pl.BlockSpec((tm, IN_WIDTH), lambda i: (i, 0)),
        compiler_params=_params(1),
        name="inproj",
    )(h, g, w_in)


def _outproj_kernel(h_ref, oa_ref, ob_ref, oc_ref, g_ref, w_ref, o_ref):
    y = _dot(oa_ref[...].astype(BF16), w_ref[0:HG_WIDTH, :])
    y = y + _dot(ob_ref[...].astype(BF16), w_ref[HG_WIDTH:HG_WIDTH + ATT_WIDTH, :])
    y = y + _dot(oc_ref[...].astype(BF16), w_ref[HG_WIDTH + ATT_WIDTH:, :])
    o_ref[...] = h_ref[...] + _rms(y, g_ref[...])


def _outproj(h, o_a, o_b, o_c, g, w_out):
    n = h.shape[0]
    tm = min(TOKEN_TILE, n)
    row = lambda w: pl.BlockSpec((tm, w), lambda i: (i, 0))
    return pl.pallas_call(
        _outproj_kernel,
        out_shape=jax.ShapeDtypeStruct((n, D_MODEL), F32),
        grid=(n // tm,),
        in_specs=[row(D_MODEL), row(HG_WIDTH), row(ATT_WIDTH), row(SG_WIDTH), _const_spec((1, D_MODEL)),
                  _const_spec(w_out.shape)],
        out_specs=row(D_MODEL),
        compiler_params=_params(1),
        name="outproj",
    )(h, o_a, o_b, o_c, g, w_out)


def _ple_kernel(h_ref, p_ref, gin_ref, gout_ref, wgate_ref, wproj_ref, o_ref):
    h = h_ref[...]
    gate = jax.nn.sigmoid(_dot(_rms(h, gin_ref[...]).astype(BF16), wgate_ref[...]))
    e = _dot(p_ref[...].astype(BF16), wproj_ref[...])
    o_ref[...] = h + _rms(e * gate, gout_ref[...])


def _ple(h, p, g_in, g_out, w_gate, w_proj):
    n = h.shape[0]
    tm = min(TOKEN_TILE, n)
    row = lambda w: pl.BlockSpec((tm, w), lambda i: (i, 0))
    return pl.pallas_call(
        _ple_kernel,
        out_shape=jax.ShapeDtypeStruct((n, D_MODEL), F32),
        grid=(n // tm,),
        in_specs=[row(D_MODEL), row(PLE_DIM), _const_spec((1, D_MODEL)), _const_spec((1, D_MODEL)),
                  _const_spec(w_gate.shape), _const_spec(w_proj.shape)],
        out_specs=row(D_MODEL),
        compiler_params=_params(1),
        name="ple",
    )(h, p, g_in, g_out, w_gate, w_proj)


def _rope_table_kernel(pos_ref, inv_ref, sign_ref, cos_ref, sin_ref):
    ang = pos_ref[...] * inv_ref[...]
    cos_ref[...] = jnp.cos(ang)
    sin_ref[...] = jnp.sin(ang) * sign_ref[...]


def _rope_tables(positions):
    n = positions.size
    half = HEAD_DIM // 2
    inv = ROPE_THETA ** (-jnp.arange(half, dtype=F32) / half)
    inv = jnp.tile(inv, LANES // half)[None, :]
    sign = jnp.asarray(np.where((np.arange(LANES) % HEAD_DIM) < half, -1.0, 1.0)[None, :], F32)
    pos = jnp.broadcast_to(positions.astype(F32).reshape(n, 1), (n, LANES))
    tm = min(TOKEN_TILE, n)
    row = pl.BlockSpec((tm, LANES), lambda i: (i, 0))
    return pl.pallas_call(
        _rope_table_kernel,
        out_shape=[jax.ShapeDtypeStruct((n, LANES), F32)] * 2,
        grid=(n // tm,),
        in_specs=[row, _const_spec((1, LANES)), _const_spec((1, LANES))],
        out_specs=[row, row],
        compiler_params=_params(1),
        name="rope_tables",
    )(pos, inv, sign)


def _swap_halves(x):
    w = x.shape[-1]
    half = HEAD_DIM // 2
    lane = lax.broadcasted_iota(jnp.int32, x.shape, 1)
    first = (lane % HEAD_DIM) < half
    return jnp.where(first, pltpu.roll(x, w - half, 1), pltpu.roll(x, half, 1))


def _attn_kernel(sink_ref, q_ref, kc_ref, kp_ref, vc_ref, vp_ref, cc_ref, cp_ref, sc_ref, sp_ref, o_ref):
    n = pl.program_id(1)
    cos_c, sin_c = cc_ref[...], sc_ref[...]
    q = q_ref[...]
    reps = ATT_WIDTH // LANES
    q = q * jnp.concatenate([cos_c] * reps, axis=1) + _swap_halves(q) * jnp.concatenate([sin_c] * reps, axis=1)
    kc = kc_ref[...]
    kc = kc * cos_c + _swap_halves(kc) * sin_c
    kp = kp_ref[...]
    kp = kp * cp_ref[...] + _swap_halves(kp) * sp_ref[...]
    k = jnp.concatenate([kp, kc], axis=0).astype(BF16)
    v = jnp.concatenate([vp_ref[...], vc_ref[...]], axis=0).astype(BF16)
    q = q.astype(BF16)

    qi = lax.broadcasted_iota(jnp.int32, (WINDOW, 2 * WINDOW), 0)
    sj = lax.broadcasted_iota(jnp.int32, (WINDOW, 2 * WINDOW), 1)
    rel = qi + WINDOW - sj
    mask = (rel >= 0) & (rel < WINDOW) & ((sj >= WINDOW) | (n > 0))

    group = ATT_HEADS // ATT_KV_HEADS
    outs = []
    for hd in range(ATT_HEADS):
        kv = hd // group
        qh = q[:, hd * HEAD_DIM:(hd + 1) * HEAD_DIM]
        kh = k[:, kv * HEAD_DIM:(kv + 1) * HEAD_DIM]
        vh = v[:, kv * HEAD_DIM:(kv + 1) * HEAD_DIM]
        s = _dot_nt(qh, kh) * (HEAD_DIM ** -0.5)
        s = jnp.where(mask, s, MASK_VALUE)
        sink = sink_ref[hd]
        m = jnp.maximum(jnp.max(s, axis=-1, keepdims=True), sink)
        pr = jnp.where(mask, jnp.exp(s - m), 0.0)
        denom = jnp.sum(pr, axis=-1, keepdims=True) + jnp.exp(sink - m)
        outs.append(_dot((pr / denom).astype(BF16), vh))
    o_ref[...] = jnp.concatenate(outs, axis=1)


def _attention(z, sinks, cos, sin, batch, seq):
    nb = seq // WINDOW
    cur = lambda b, n: b * nb + n
    prev = lambda b, n: b * nb + jnp.maximum(n - 1, 0)
    kcol, vcol = COL_AK // ATT_KV_WIDTH, COL_AV // ATT_KV_WIDTH
    blk = lambda w, rowf, col: pl.BlockSpec((WINDOW, w), lambda b, n: (rowf(b, n), col))
    return pl.pallas_call(
        _attn_kernel,
        out_shape=jax.ShapeDtypeStruct((batch * seq, ATT_WIDTH), F32),
        grid=(batch, nb),
        in_specs=[pl.BlockSpec(memory_space=pltpu.SMEM),
                  blk(ATT_WIDTH, cur, COL_AQ // ATT_WIDTH),
                  blk(ATT_KV_WIDTH, cur, kcol), blk(ATT_KV_WIDTH, prev, kcol),
                  blk(ATT_KV_WIDTH, cur, vcol), blk(ATT_KV_WIDTH, prev, vcol),
                  blk(LANES, cur, 0), blk(LANES, prev, 0), blk(LANES, cur, 0), blk(LANES, prev, 0)],
        out_specs=pl.BlockSpec((WINDOW, ATT_WIDTH), lambda b, n: (cur(b, n), 0)),
        compiler_params=_params(2),
        name="swa_attention",
    )(sinks, z, z, z, z, z, cos, cos, sin, sin)


def _sgu_kernel(u_ref, v_ref, lng_ref, w_ref, bias_ref, o_ref):
    u = jax.nn.gelu(u_ref[...])
    v = jax.nn.gelu(v_ref[...])
    mu = jnp.mean(v, axis=-1, keepdims=True)
    vc = v - mu
    v = (vc * lax.rsqrt(jnp.mean(vc * vc, axis=-1, keepdims=True) + EPS) * lng_ref[...]).astype(BF16)
    ti = lax.broadcasted_iota(jnp.int32, (SG_CHUNK, SG_CHUNK), 0)
    si = lax.broadcasted_iota(jnp.int32, (SG_CHUNK, SG_CHUNK), 1)
    causal = si <= ti
    ws = [jnp.where(causal, w_ref[g], 0.0).astype(BF16) for g in range(SG_GROUPS)]
    lane_group = lax.broadcasted_iota(jnp.int32, (SG_CHUNK, SG_WIDTH), 1) // HEAD_DIM
    bias = bias_ref[...]
    for c in range(u.shape[0] // SG_CHUNK):
        rows = slice(c * SG_CHUNK, (c + 1) * SG_CHUNK)
        vch = v[rows]
        mix = bias
        for g in range(SG_GROUPS):
            mix = mix + jnp.where(lane_group == g, _dot(ws[g], vch), 0.0)
        o_ref[rows, :] = u[rows] * mix


def _sgu(z, ln_g, w_s, bias_tab):
    n = z.shape[0]
    tm = min(TOKEN_TILE, n)
    return pl.pallas_call(
        _sgu_kernel,
        out_shape=jax.ShapeDtypeStruct((n, SG_WIDTH), F32),
        grid=(n // tm,),
        in_specs=[pl.BlockSpec((tm, SG_WIDTH), lambda i: (i, COL_SU // SG_WIDTH)),
                  pl.BlockSpec((tm, SG_WIDTH), lambda i: (i, COL_SV // SG_WIDTH)),
                  _const_spec((1, SG_WIDTH)), _const_spec(w_s.shape), _const_spec(bias_tab.shape)],
        out_specs=pl.BlockSpec((tm, SG_WIDTH), lambda i: (i, 0)),
        compiler_params=_params(1),
        name="spatial_gating",
    )(z, z, ln_g, w_s, bias_tab)


def _group_cumsum(x):
    n = x.shape[0]
    r = lax.broadcasted_iota(jnp.int32, x.shape, 0) % HG_SUB
    s = 1
    while s < HG_SUB:
        x = x + jnp.where(r >= s, pltpu.roll(x, s, 0), 0.0)
        s *= 2
    return x


def _group_last(x):
    n = x.shape[0]
    r = lax.broadcasted_iota(jnp.int32, x.shape, 0) % HG_SUB
    t = jnp.where(r == HG_SUB - 1, x, 0.0)
    s = 1
    while s < HG_SUB:
        t = t + pltpu.roll(t, n - s, 0)
        s *= 2
    return t


def _head_block_mask(shape):
    r = lax.broadcasted_iota(jnp.int32, shape, 0) // HEAD_DIM
    c = lax.broadcasted_iota(jnp.int32, shape, 1) // HEAD_DIM
    return r == c


def _dot_tn(a, b):
    return lax.dot_general(a, b, (((0,), (0,)), ((), ())), preferred_element_type=F32)


def _hgrn_kernel(z_ref, lbl_ref, gn_ref, o_ref, st_ref, qd_ref, kd_ref, vb_ref, t_ref, raw_ref, *, layer):
    @pl.when(pl.program_id(1) == 0)
    def _():
        st_ref[...] = jnp.zeros_like(st_ref)

    w = HG_WIDTH
    tile = z_ref.shape[0]
    groups = tile // HG_SUB
    half = HG_SUB // 2
    piece = groups * half
    q = z_ref[:, 0:w]
    x = z_ref[:, w:2 * w]
    v = z_ref[:, 2 * w:3 * w]

    lg = lbl_ref[...]
    e = jnp.exp(lg - jnp.max(lg, axis=0, keepdims=True))
    probs = e / jnp.sum(e, axis=0, keepdims=True)
    lb = jnp.sum(probs[0:layer + 1], axis=0, keepdims=True) - probs[0:1]
    lb_floor = jnp.maximum(lb, LB_FLOOR)

    t = jnp.exp(-jnp.abs(x))
    r = 1.0 / (1.0 + t)
    tr = t * r
    pos = x >= 0.0
    logf = jnp.log(lb_floor + (1.0 - lb) * jnp.where(pos, r, tr))
    k = (1.0 - lb) * jnp.where(pos, tr, r) + (lb - lb_floor)

    b = _group_cumsum(logf)
    b_last = _group_last(b)
    qd_ref[...] = (q * jnp.exp(b)).astype(BF16)
    kd_ref[...] = (k * jnp.exp(b_last - b)).astype(BF16)
    vb_ref[...] = v.astype(BF16)

    grp = lambda a: a.reshape(groups, HG_SUB, w)
    b3, q3, k3, v3 = grp(b * LOG2E), grp(q), grp(k), grp(v)
    b_lo, b_hi = b3[:, :half], b3[:, half:]
    q_lo, q_hi = q3[:, :half], q3[:, half:]
    row = lax.broadcasted_iota(jnp.int32, (groups, half, w), 1)
    pair = 2 * HEAD_DIM
    blockdiag = _head_block_mask((pair, pair))
    st = [st_ref[p] for p in range(w // pair)]

    def advance(m):
        rows = slice(m * HG_SUB, (m + 1) * HG_SUB)
        decay = jnp.exp(b_last[m * HG_SUB:m * HG_SUB + 1, :])
        for p in range(w // pair):
            lanes = slice(p * pair, (p + 1) * pair)
            raw_ref[rows, lanes] = _dot_nt(qd_ref[rows, lanes], st[p].astype(BF16))
            upd = _dot_tn(vb_ref[rows, lanes], kd_ref[rows, lanes])
            st[p] = st[p] * decay[:, lanes] + jnp.where(blockdiag, upd, 0.0)

    idx = 0
    for j in range(HG_SUB):
        for m in range(j * groups // HG_SUB, (j + 1) * groups // HG_SUB):
            advance(m)
        bj = b3[:, j:j + 1, :]
        kj = k3[:, j:j + 1, :]
        if j < half:
            dec = jnp.exp2(jnp.where(row >= j, b_lo - bj, MASK_VALUE))
            t_ref[idx * piece:(idx + 1) * piece, :] = (q_lo * kj * dec).reshape(piece, w).astype(BF16)
            idx += 1
            dec = jnp.exp2(b_hi - bj)
        else:
            dec = jnp.exp2(jnp.where(row >= j - half, b_hi - bj, MASK_VALUE))
        t_ref[idx * piece:(idx + 1) * piece, :] = (q_hi * kj * dec).reshape(piece, w).astype(BF16)
        idx += 1
    ones_bd = jnp.where(_head_block_mask((w, w)), 1.0, 0.0).astype(BF16)
    att = _dot(t_ref[...], ones_bd)
    o_lo = jnp.zeros((groups, half, w), F32)
    o_hi = jnp.zeros((groups, half, w), F32)
    idx = 0
    for j in range(HG_SUB):
        vj = v3[:, j:j + 1, :]
        if j < half:
            o_lo = o_lo + att[idx * piece:(idx + 1) * piece].reshape(groups, half, w) * vj
            idx += 1
        o_hi = o_hi + att[idx * piece:(idx + 1) * piece].reshape(groups, half, w) * vj
        idx += 1
    o_intra = jnp.concatenate([o_lo, o_hi], axis=1).reshape(tile, w)
    for p in range(w // pair):
        st_ref[p] = st[p]

    o = raw_ref[...] + o_intra
    sq = o * o
    hi = sq.astype(BF16)
    lo = (sq - hi.astype(F32)).astype(BF16)
    ms = (_dot(hi, ones_bd) + _dot(lo, ones_bd)) * (1.0 / HEAD_DIM)
    gate = jax.nn.sigmoid(z_ref[:, 3 * w:4 * w])
    o_ref[...] = o * lax.rsqrt(ms + EPS) * gn_ref[...] * gate


def _hgrn(z, lb_logits, norm_gain, layer, batch, seq):
    nt = seq // HG_TILE
    w = HG_WIDTH
    pair = 2 * HEAD_DIM
    pieces = HG_SUB + HG_SUB // 2
    return pl.pallas_call(
        functools.partial(_hgrn_kernel, layer=layer),
        out_shape=jax.ShapeDtypeStruct((batch * seq, w), F32),
        grid=(batch, nt),
        in_specs=[pl.BlockSpec((HG_TILE, 4 * w), lambda b, t: (b * nt + t, COL_HG)),
                  _const_spec(lb_logits.shape), _const_spec((1, w))],
        out_specs=pl.BlockSpec((HG_TILE, w), lambda b, t: (b * nt + t, 0)),
        scratch_shapes=[pltpu.VMEM((w // pair, pair, pair), F32),
                        pltpu.VMEM((HG_TILE, w), BF16),
                        pltpu.VMEM((HG_TILE, w), BF16),
                        pltpu.VMEM((HG_TILE, w), BF16),
                        pltpu.VMEM((pieces * HG_TILE // 2, w), BF16),
                        pltpu.VMEM((HG_TILE, w), F32)],
        compiler_params=_params(2, "arbitrary"),
        name="hgrn2",
    )(z, lb_logits, norm_gain)


def kernel(x, p, positions, norm_gains, w_in, w_out, ffn1_gate_up, ffn1_down, ffn2_gate_up, ffn2_down,
           hgrn_lb_logits, hgrn_norm_gain, attn_sinks, sg_ln_gain, sg_spatial_w, sg_spatial_b, ple_proj, ple_gate):
    batch, seq, _ = x.shape
    depth = norm_gains.shape[0]
    n = batch * seq
    h = x.reshape(n, D_MODEL)
    cos, sin = _rope_tables(positions)

    def split_gate_up(w):
        chunks = D_FF // FF_CHUNK
        to_chunks = lambda m: m.reshape(D_MODEL, chunks, FF_CHUNK).transpose(1, 0, 2).astype(BF16)
        return to_chunks(w[:, :D_FF]), to_chunks(w[:, D_FF:])

    for l in range(depth):
        gains = norm_gains[l].reshape(norm_gains.shape[1], 1, D_MODEL)
        wg1, wu1 = split_gate_up(ffn1_gate_up[l])
        wg2, wu2 = split_gate_up(ffn2_gate_up[l])
        h = _ffn(h, gains[0], gains[1], wg1, wu1, ffn1_down[l].astype(BF16))
        z = _inproj(h, gains[2], w_in[l].astype(BF16))
        o_a = _hgrn(z, hgrn_lb_logits, jnp.tile(hgrn_norm_gain[l], HG_HEADS)[None, :], l, batch, seq)
        o_b = _attention(z, attn_sinks[l], cos, sin, batch, seq)
        bias_tab = jnp.repeat(sg_spatial_b[l].T, HEAD_DIM, axis=1)
        o_c = _sgu(z, sg_ln_gain[l][None, :], sg_spatial_w[l], bias_tab)
        h = _outproj(h, o_a, o_b, o_c, gains[3], w_out[l].astype(BF16))
        h = _ffn(h, gains[4], gains[5], wg2, wu2, ffn2_down[l].astype(BF16))
        h = _ple(h, p[l].reshape(n, PLE_DIM), gains[6], gains[7], ple_gate[l].astype(BF16), ple_proj[l].astype(BF16))
    return h.reshape(batch, seq, D_MODEL)
```

```python
import functools

import numpy as np
import jax
import jax.numpy as jnp
from jax import lax
from jax.experimental import pallas as pl
from jax.experimental.pallas import tpu as pltpu

F32 = jnp.float32
BF16 = jnp.bfloat16

D_MODEL = 1024
HEAD_DIM = 64
HG_HEADS = 4
HG_WIDTH = HG_HEADS * HEAD_DIM
ATT_HEADS = 8
ATT_KV_HEADS = 2
ATT_WIDTH = ATT_HEADS * HEAD_DIM
ATT_KV_WIDTH = ATT_KV_HEADS * HEAD_DIM
WINDOW = 128
ROPE_THETA = 10000.0
SG_GROUPS = 4
SG_WIDTH = SG_GROUPS * HEAD_DIM
SG_CHUNK = 128
IN_WIDTH = 4 * HG_WIDTH + ATT_WIDTH + 2 * ATT_KV_WIDTH + 2 * SG_WIDTH
D_FF = 2816
PLE_DIM = 256
EPS = 1e-6
MASK_VALUE = -1e30
LB_FLOOR = 1e-30
LOG2E = 1.4426950408889634

COL_HG = 0
COL_AQ = 4 * HG_WIDTH
COL_AK = COL_AQ + ATT_WIDTH
COL_AV = COL_AK + ATT_KV_WIDTH
COL_SU = COL_AV + ATT_KV_WIDTH
COL_SV = COL_SU + SG_WIDTH

LANES = 128
TOKEN_TILE = 512
FF_CHUNK = 256
ATT_TILE = 512
HG_TILE = 256
HG_SUB = 16
VMEM_LIMIT = 56 * 1024 * 1024


def _const_spec(shape):
    nd = len(shape)
    return pl.BlockSpec(shape, lambda *_: (0,) * nd, pipeline_mode=pl.Buffered(1))


def _rms(x, g):
    ms = jnp.mean(x * x, axis=-1, keepdims=True)
    return x * lax.rsqrt(ms + EPS) * g


def _dot(a, b):
    return jnp.dot(a, b, preferred_element_type=F32)


def _dot_nt(a, b):
    return lax.dot_general(a, b, (((1,), (1,)), ((), ())), preferred_element_type=F32)


def _params(n_axes, semantics="parallel"):
    return pltpu.CompilerParams(dimension_semantics=(semantics,) * n_axes, vmem_limit_bytes=VMEM_LIMIT)


def _swiglu_half_step(h, gpre, gpost, wgu_ref, wd_ref, act_ref):
    x = _rms(h, gpre).astype(BF16)
    for c in range(D_FF // FF_CHUNK):
        g = _dot(x, wgu_ref[:, c * FF_CHUNK:(c + 1) * FF_CHUNK])
        u = _dot(x, wgu_ref[:, D_FF + c * FF_CHUNK:D_FF + (c + 1) * FF_CHUNK])
        act_ref[:, c * FF_CHUNK:(c + 1) * FF_CHUNK] = (g * jax.nn.sigmoid(g) * u).astype(BF16)
    y = _dot(act_ref[...], wd_ref[...])
    return 0.5 * _rms(y, gpost)


def _premix_kernel(h_ref, g_ref, wgu_ref, wd_ref, win_ref, o_ref, z_ref, act_ref):
    h = h_ref[...]
    h = h + _swiglu_half_step(h, g_ref[0:1, :], g_ref[1:2, :], wgu_ref, wd_ref, act_ref)
    o_ref[...] = h
    z_ref[...] = _dot(_rms(h, g_ref[2:3, :]).astype(BF16), win_ref[...])


def _premix(h, gains, wgu, wd, w_in):
    n = h.shape[0]
    tm = min(TOKEN_TILE, n)
    row = lambda w: pl.BlockSpec((tm, w), lambda i: (i, 0))
    return pl.pallas_call(
        _premix_kernel,
        out_shape=[jax.ShapeDtypeStruct((n, D_MODEL), F32), jax.ShapeDtypeStruct((n, IN_WIDTH), F32)],
        grid=(n // tm,),
        in_specs=[row(D_MODEL), _const_spec(gains.shape), _const_spec(wgu.shape), _const_spec(wd.shape),
                  _const_spec(w_in.shape)],
        out_specs=[row(D_MODEL), row(IN_WIDTH)],
        scratch_shapes=[pltpu.VMEM((tm, D_FF), BF16)],
        compiler_params=_params(1),
        name="premix",
    )(h, gains, wgu, wd, w_in)


def _postmix_kernel(h_ref, oa_ref, ob_ref, oc_ref, p_ref, g_ref, wout_ref, wgu_ref, wd_ref, wgate_ref, wproj_ref,
                    o_ref, act_ref):
    y = _dot(oa_ref[...].astype(BF16), wout_ref[0:HG_WIDTH, :])
    y = y + _dot(ob_ref[...].astype(BF16), wout_ref[HG_WIDTH:HG_WIDTH + ATT_WIDTH, :])
    y = y + _dot(oc_ref[...].astype(BF16), wout_ref[HG_WIDTH + ATT_WIDTH:, :])
    h = h_ref[...] + _rms(y, g_ref[3:4, :])
    h = h + _swiglu_half_step(h, g_ref[4:5, :], g_ref[5:6, :], wgu_ref, wd_ref, act_ref)
    gate = jax.nn.sigmoid(_dot(_rms(h, g_ref[6:7, :]).astype(BF16), wgate_ref[...]))
    e = _dot(p_ref[...].astype(BF16), wproj_ref[...])
    o_ref[...] = h + _rms(e * gate, g_ref[7:8, :])


def _postmix(h, o_a, o_b, o_c, p, gains, w_out, wgu, wd, w_gate, w_proj):
    n = h.shape[0]
    tm = min(TOKEN_TILE, n)
    row = lambda w: pl.BlockSpec((tm, w), lambda i: (i, 0))
    return pl.pallas_call(
        _postmix_kernel,
        out_shape=jax.ShapeDtypeStruct((n, D_MODEL), F32),
        grid=(n // tm,),
        in_specs=[row(D_MODEL), row(HG_WIDTH), row(ATT_WIDTH), row(SG_WIDTH), row(PLE_DIM), _const_spec(gains.shape),
                  _const_spec(w_out.shape), _const_spec(wgu.shape), _const_spec(wd.shape),
                  _const_spec(w_gate.shape), _const_spec(w_proj.shape)],
        out_specs=row(D_MODEL),
        scratch_shapes=[pltpu.VMEM((tm, D_FF), BF16)],
        compiler_params=_params(1),
        name="postmix",
    )(h, o_a, o_b, o_c, p, gains, w_out, wgu, wd, w_gate, w_proj)


def _rope_table_kernel(pos_ref, inv_ref, sign_ref, cos_ref, sin_ref):
    ang = pos_ref[...] * inv_ref[...]
    cos_ref[...] = jnp.cos(ang)
    sin_ref[...] = jnp.sin(ang) * sign_ref[...]


def _rope_tables(positions):
    n = positions.size
    half = HEAD_DIM // 2
    inv = ROPE_THETA ** (-jnp.arange(half, dtype=F32) / half)
    inv = jnp.tile(inv, LANES // half)[None, :]
    sign = jnp.asarray(np.where((np.arange(LANES) % HEAD_DIM) < half, -1.0, 1.0)[None, :], F32)
    pos = jnp.broadcast_to(positions.astype(F32).reshape(n, 1), (n, LANES))
    tm = min(TOKEN_TILE, n)
    row = pl.BlockSpec((tm, LANES), lambda i: (i, 0))
    return pl.pallas_call(
        _rope_table_kernel,
        out_shape=[jax.ShapeDtypeStruct((n, LANES), F32)] * 2,
        grid=(n // tm,),
        in_specs=[row, _const_spec((1, LANES)), _const_spec((1, LANES))],
        out_specs=[row, row],
        compiler_params=_params(1),
        name="rope_tables",
    )(pos, inv, sign)


def _swap_halves(x):
    w = x.shape[-1]
    half = HEAD_DIM // 2
    lane = lax.broadcasted_iota(jnp.int32, x.shape, 1)
    first = (lane % HEAD_DIM) < half
    return jnp.where(first, pltpu.roll(x, w - half, 1), pltpu.roll(x, half, 1))


def _attn_kernel(sink_ref, q_ref, kc_ref, kp_ref, vc_ref, vp_ref, cc_ref, cp_ref, sc_ref, sp_ref, o_ref):
    first_tile = pl.program_id(1) == 0
    blocks = q_ref.shape[0] // WINDOW
    group = ATT_HEADS // ATT_KV_HEADS
    cos_c, sin_c = cc_ref[...], sc_ref[...]
    q = q_ref[...]
    reps = ATT_WIDTH // LANES
    q = q * jnp.concatenate([cos_c] * reps, axis=1) + _swap_halves(q) * jnp.concatenate([sin_c] * reps, axis=1)
    kc = kc_ref[...]
    kc = kc * cos_c + _swap_halves(kc) * sin_c
    kp = kp_ref[...]
    kp = kp * cp_ref[...] + _swap_halves(kp) * sp_ref[...]
    k = jnp.concatenate([kp, kc], axis=0).astype(BF16)
    v = jnp.concatenate([vp_ref[...], vc_ref[...]], axis=0).astype(BF16)
    q = q.astype(BF16)

    rows = group * WINDOW
    qi = lax.broadcasted_iota(jnp.int32, (rows, 2 * WINDOW), 0) % WINDOW
    sj = lax.broadcasted_iota(jnp.int32, (rows, 2 * WINDOW), 1)
    rel = qi + WINDOW - sj
    band = (rel >= 0) & (rel < WINDOW)
    band_first = band & ((sj >= WINDOW) | jnp.logical_not(first_tile))

    outs = [[None] * ATT_HEADS for _ in range(blocks)]
    for kv in range(ATT_KV_HEADS):
        lanes = slice(kv * HEAD_DIM, (kv + 1) * HEAD_DIM)
        sink = jnp.concatenate([jnp.full((WINDOW, 1), sink_ref[kv * group + g], F32) for g in range(group)], axis=0)
        for r in range(blocks):
            mask = band_first if r == 0 else band
            qs = jnp.concatenate([q[r * WINDOW:(r + 1) * WINDOW, (kv * group + g) * HEAD_DIM:(kv * group + g + 1) * HEAD_DIM]
                                  for g in range(group)], axis=0)
            keys = slice(r * WINDOW, (r + 2) * WINDOW)
            s = _dot_nt(qs, k[keys, lanes]) * (HEAD_DIM ** -0.5)
            s = jnp.where(mask, s, MASK_VALUE)
            m = jnp.maximum(jnp.max(s, axis=-1, keepdims=True), sink)
            pr = jnp.where(mask, jnp.exp(s - m), 0.0)
            denom = jnp.sum(pr, axis=-1, keepdims=True) + jnp.exp(sink - m)
            o = _dot((pr * (1.0 / denom)).astype(BF16), v[keys, lanes])
            for g in range(group):
                outs[r][kv * group + g] = o[g * WINDOW:(g + 1) * WINDOW]
    for r in range(blocks):
        o_ref[r * WINDOW:(r + 1) * WINDOW, :] = jnp.concatenate(outs[r], axis=1)


def _attention(z, sinks, cos, sin, batch, seq):
    tile = min(ATT_TILE, seq)
    nt = seq // tile
    per = tile // WINDOW
    cur = lambda b, n: (b * nt + n)
    prev = lambda b, n: (b * nt + n) * per - jnp.minimum(n, 1)
    kcol, vcol = COL_AK // ATT_KV_WIDTH, COL_AV // ATT_KV_WIDTH
    tile_blk = lambda w, col: pl.BlockSpec((tile, w), lambda b, n: (cur(b, n), col))
    prev_blk = lambda w, col: pl.BlockSpec((WINDOW, w), lambda b, n: (prev(b, n), col))
    return pl.pallas_call(
        _attn_kernel,
        out_shape=jax.ShapeDtypeStruct((batch * seq, ATT_WIDTH), F32),
        grid=(batch, nt),
        in_specs=[pl.BlockSpec(memory_space=pltpu.SMEM),
                  tile_blk(ATT_WIDTH, COL_AQ // ATT_WIDTH),
                  tile_blk(ATT_KV_WIDTH, kcol), prev_blk(ATT_KV_WIDTH, kcol),
                  tile_blk(ATT_KV_WIDTH, vcol), prev_blk(ATT_KV_WIDTH, vcol),
                  tile_blk(LANES, 0), prev_blk(LANES, 0), tile_blk(LANES, 0), prev_blk(LANES, 0)],
        out_specs=pl.BlockSpec((tile, ATT_WIDTH), lambda b, n: (cur(b, n), 0)),
        compiler_params=_params(2),
        name="swa_attention",
    )(sinks, z, z, z, z, z, cos, cos, sin, sin)


def _sgu_kernel(u_ref, v_ref, lng_ref, w_ref, bias_ref, o_ref):
    u = jax.nn.gelu(u_ref[...])
    v = jax.nn.gelu(v_ref[...])
    mu = jnp.mean(v, axis=-1, keepdims=True)
    vc = v - mu
    v = (vc * lax.rsqrt(jnp.mean(vc * vc, axis=-1, keepdims=True) + EPS) * lng_ref[...]).astype(BF16)
    ti = lax.broadcasted_iota(jnp.int32, (SG_CHUNK, SG_CHUNK), 0)
    si = lax.broadcasted_iota(jnp.int32, (SG_CHUNK, SG_CHUNK), 1)
    causal = si <= ti
    ws = [jnp.where(causal, w_ref[g], 0.0).astype(BF16) for g in range(SG_GROUPS)]
    lane_group = lax.broadcasted_iota(jnp.int32, (SG_CHUNK, SG_WIDTH), 1) // HEAD_DIM
    bias = bias_ref[...]
    for c in range(u.shape[0] // SG_CHUNK):
        rows = slice(c * SG_CHUNK, (c + 1) * SG_CHUNK)
        vch = v[rows]
        mix = bias
        for g in range(SG_GROUPS):
            mix = mix + jnp.where(lane_group == g, _dot(ws[g], vch), 0.0)
        o_ref[rows, :] = u[rows] * mix


def _sgu(z, ln_g, w_s, bias_tab):
    n = z.shape[0]
    tm = min(TOKEN_TILE, n)
    return pl.pallas_call(
        _sgu_kernel,
        out_shape=jax.ShapeDtypeStruct((n, SG_WIDTH), F32),
        grid=(n // tm,),
        in_specs=[pl.BlockSpec((tm, SG_WIDTH), lambda i: (i, COL_SU // SG_WIDTH)),
                  pl.BlockSpec((tm, SG_WIDTH), lambda i: (i, COL_SV // SG_WIDTH)),
                  _const_spec((1, SG_WIDTH)), _const_spec(w_s.shape), _const_spec(bias_tab.shape)],
        out_specs=pl.BlockSpec((tm, SG_WIDTH), lambda i: (i, 0)),
        compiler_params=_params(1),
        name="spatial_gating",
    )(z, z, ln_g, w_s, bias_tab)


def _group_cumsum(x):
    r = lax.broadcasted_iota(jnp.int32, x.shape, 0) % HG_SUB
    s = 1
    while s < HG_SUB:
        x = x + jnp.where(r >= s, pltpu.roll(x, s, 0), 0.0)
        s *= 2
    return x


def _group_last(x):
    n = x.shape[0]
    r = lax.broadcasted_iota(jnp.int32, x.shape, 0) % HG_SUB
    t = jnp.where(r == HG_SUB - 1, x, 0.0)
    s = 1
    while s < HG_SUB:
        t = t + pltpu.roll(t, n - s, 0)
        s *= 2
    return t


def _head_block_mask(shape):
    r = lax.broadcasted_iota(jnp.int32, shape, 0) // HEAD_DIM
    c = lax.broadcasted_iota(jnp.int32, shape, 1) // HEAD_DIM
    return r == c


def _dot_tn(a, b):
    return lax.dot_general(a, b, (((0,), (0,)), ((), ())), preferred_element_type=F32)


def _hgrn_kernel(z_ref, lbl_ref, gn_ref, o_ref, st_ref, qd_ref, kd_ref, vb_ref, t_ref, raw_ref, *, layer):
    @pl.when(pl.program_id(1) == 0)
    def _():
        st_ref[...] = jnp.zeros_like(st_ref)

    w = HG_WIDTH
    tile = z_ref.shape[0]
    groups = tile // HG_SUB
    half = HG_SUB // 2
    piece = groups * half
    q = z_ref[:, 0:w]
    x = z_ref[:, w:2 * w]
    v = z_ref[:, 2 * w:3 * w]

    lg = lbl_ref[...]
    e = jnp.exp(lg - jnp.max(lg, axis=0, keepdims=True))
    probs = e / jnp.sum(e, axis=0, keepdims=True)
    lb = jnp.sum(probs[0:layer + 1], axis=0, keepdims=True) - probs[0:1]
    lb_floor = jnp.maximum(lb, LB_FLOOR)

    t = jnp.exp(-jnp.abs(x))
    r = 1.0 / (1.0 + t)
    tr = t * r
    pos = x >= 0.0
    logf = jnp.log(lb_floor + (1.0 - lb) * jnp.where(pos, r, tr))
    k = (1.0 - lb) * jnp.where(pos, tr, r) + (lb - lb_floor)

    b = _group_cumsum(logf)
    b_last = _group_last(b)
    qd_ref[...] = (q * jnp.exp(b)).astype(BF16)
    kd_ref[...] = (k * jnp.exp(b_last - b)).astype(BF16)
    vb_ref[...] = v.astype(BF16)

    grp = lambda a: a.reshape(groups, HG_SUB, w)
    b3, q3, k3, v3 = grp(b * LOG2E), grp(q), grp(k), grp(v)
    b_lo, b_hi = b3[:, :half], b3[:, half:]
    q_lo, q_hi = q3[:, :half], q3[:, half:]
    row = lax.broadcasted_iota(jnp.int32, (groups, half, w), 1)
    pair = 2 * HEAD_DIM
    blockdiag = _head_block_mask((pair, pair))
    st = [st_ref[p] for p in range(w // pair)]

    def advance(m):
        rows = slice(m * HG_SUB, (m + 1) * HG_SUB)
        decay = jnp.exp(b_last[m * HG_SUB:m * HG_SUB + 1, :])
        for p in range(w // pair):
            lanes = slice(p * pair, (p + 1) * pair)
            raw_ref[rows, lanes] = _dot_nt(qd_ref[rows, lanes], st[p].astype(BF16))
            upd = _dot_tn(vb_ref[rows, lanes], kd_ref[rows, lanes])
            st[p] = st[p] * decay[:, lanes] + jnp.where(blockdiag, upd, 0.0)

    idx = 0
    for j in range(HG_SUB):
        for m in range(j * groups // HG_SUB, (j + 1) * groups // HG_SUB):
            advance(m)
        bj = b3[:, j:j + 1, :]
        kj = k3[:, j:j + 1, :]
        if j < half:
            dec = jnp.exp2(jnp.where(row >= j, b_lo - bj, MASK_VALUE))
            t_ref[idx * piece:(idx + 1) * piece, :] = (q_lo * kj * dec).reshape(piece, w).astype(BF16)
            idx += 1
            dec = jnp.exp2(b_hi - bj)
        else:
            dec = jnp.exp2(jnp.where(row >= j - half, b_hi - bj, MASK_VALUE))
        t_ref[idx * piece:(idx + 1) * piece, :] = (q_hi * kj * dec).reshape(piece, w).astype(BF16)
        idx += 1
    ones_bd = jnp.where(_head_block_mask((w, w)), 1.0, 0.0).astype(BF16)
    att = _dot(t_ref[...], ones_bd)
    o_lo = jnp.zeros((groups, half, w), F32)
    o_hi = jnp.zeros((groups, half, w), F32)
    idx = 0
    for j in range(HG_SUB):
        vj = v3[:, j:j + 1, :]
        if j < half:
            o_lo = o_lo + att[idx * piece:(idx + 1) * piece].reshape(groups, half, w) * vj
            idx += 1
        o_hi = o_hi + att[idx * piece:(idx + 1) * piece].reshape(groups, half, w) * vj
        idx += 1
    o_intra = jnp.concatenate([o_lo, o_hi], axis=1).reshape(tile, w)
    for p in range(w // pair):
        st_ref[p] = st[p]

    o = raw_ref[...] + o_intra
    sq = o * o
    hi = sq.astype(BF16)
    lo = (sq - hi.astype(F32)).astype(BF16)
    ms = (_dot(hi, ones_bd) + _dot(lo, ones_bd)) * (1.0 / HEAD_DIM)
    gate = jax.nn.sigmoid(z_ref[:, 3 * w:4 * w])
    o_ref[...] = o * lax.rsqrt(ms + EPS) * gn_ref[...] * gate


def _hgrn(z, lb_logits, norm_gain, layer, batch, seq):
    nt = seq // HG_TILE
    w = HG_WIDTH
    pair = 2 * HEAD_DIM
    pieces = HG_SUB + HG_SUB // 2
    return pl.pallas_call(
        functools.partial(_hgrn_kernel, layer=layer),
        out_shape=jax.ShapeDtypeStruct((batch * seq, w), F32),
        grid=(batch, nt),
        in_specs=[pl.BlockSpec((HG_TILE, 4 * w), lambda b, t: (b * nt + t, COL_HG)),
                  _const_spec(lb_logits.shape), _const_spec((1, w))],
        out_specs=pl.BlockSpec((HG_TILE, w), lambda b, t: (b * nt + t, 0)),
        scratch_shapes=[pltpu.VMEM((w // pair, pair, pair), F32),
                        pltpu.VMEM((HG_TILE, w), BF16),
                        pltpu.VMEM((HG_TILE, w), BF16),
                        pltpu.VMEM((HG_TILE, w), BF16),
                        pltpu.VMEM((pieces * HG_TILE // 2, w), BF16),
                        pltpu.VMEM((HG_TILE, w), F32)],
        compiler_params=_params(2, "arbitrary"),
        name="hgrn2",
    )(z, lb_logits, norm_gain)


def kernel(x, p, positions, norm_gains, w_in, w_out, ffn1_gate_up, ffn1_down, ffn2_gate_up, ffn2_down,
           hgrn_lb_logits, hgrn_norm_gain, attn_sinks, sg_ln_gain, sg_spatial_w, sg_spatial_b, ple_proj, ple_gate):
    batch, seq, _ = x.shape
    depth = norm_gains.shape[0]
    n = batch * seq
    h = x.reshape(n, D_MODEL)
    cos, sin = _rope_tables(positions)

    bf = lambda w: w.astype(BF16)
    for l in range(depth):
        gains = norm_gains[l]
        h, z = _premix(h, gains, bf(ffn1_gate_up[l]), bf(ffn1_down[l]), bf(w_in[l]))
        o_a = _hgrn(z, hgrn_lb_logits, jnp.tile(hgrn_norm_gain[l], HG_HEADS)[None, :], l, batch, seq)
        o_b = _attention(z, attn_sinks[l], cos, sin, batch, seq)
        bias_tab = jnp.repeat(sg_spatial_b[l].T, HEAD_DIM, axis=1)
        o_c = _sgu(z, sg_ln_gain[l][None, :], sg_spatial_w[l], bias_tab)
        h = _postmix(h, o_a, o_b, o_c, p[l].reshape(n, PLE_DIM), gains, bf(w_out[l]), bf(ffn2_gate_up[l]),
                     bf(ffn2_down[l]), bf(ple_gate[l]), bf(ple_proj[l]))
    return h.reshape(batch, seq, D_MODEL)
```

```python
import functools

import numpy as np
import jax
import jax.numpy as jnp
from jax import lax
from jax.experimental import pallas as pl
from jax.experimental.pallas import tpu as pltpu

F32 = jnp.float32
BF16 = jnp.bfloat16

D_MODEL = 1024
HEAD_DIM = 64
HG_HEADS = 4
HG_WIDTH = HG_HEADS * HEAD_DIM
ATT_HEADS = 8
ATT_KV_HEADS = 2
ATT_WIDTH = ATT_HEADS * HEAD_DIM
ATT_KV_WIDTH = ATT_KV_HEADS * HEAD_DIM
WINDOW = 128
ROPE_THETA = 10000.0
SG_GROUPS = 4
SG_WIDTH = SG_GROUPS * HEAD_DIM
SG_CHUNK = 128
IN_WIDTH = 4 * HG_WIDTH + ATT_WIDTH + 2 * ATT_KV_WIDTH + 2 * SG_WIDTH
D_FF = 2816
PLE_DIM = 256
EPS = 1e-6
MASK_VALUE = -1e30
LB_FLOOR = 1e-30
LOG2E = 1.4426950408889634

COL_HG = 0
COL_AQ = 4 * HG_WIDTH
COL_AK = COL_AQ + ATT_WIDTH
COL_AV = COL_AK + ATT_KV_WIDTH
COL_SU = COL_AV + ATT_KV_WIDTH
COL_SV = COL_SU + SG_WIDTH

LANES = 128
TOKEN_TILE = 512
FF_CHUNK = 256
ATT_TILE = 512
HG_TILE = 256
HG_SUB = 16
VMEM_LIMIT = 56 * 1024 * 1024


def _const_spec(shape):
    nd = len(shape)
    return pl.BlockSpec(shape, lambda *_: (0,) * nd, pipeline_mode=pl.Buffered(1))


def _rms(x, g):
    ms = jnp.mean(x * x, axis=-1, keepdims=True)
    return x * lax.rsqrt(ms + EPS) * g


def _dot(a, b):
    return jnp.dot(a, b, preferred_element_type=F32)


def _dot_nt(a, b):
    return lax.dot_general(a, b, (((1,), (1,)), ((), ())), preferred_element_type=F32)


def _params(n_axes, semantics="parallel"):
    return pltpu.CompilerParams(dimension_semantics=(semantics,) * n_axes, vmem_limit_bytes=VMEM_LIMIT)


def _layer_spec(a, layer):
    return pl.BlockSpec((1,) + a.shape[1:], lambda i: (layer,) + (0,) * (a.ndim - 1), pipeline_mode=pl.Buffered(1))


def _premix_kernel(sink_ref, h_ref, g_ref, wgu_ref, wd_ref, win_ref, cos_ref, sin_ref,
                   o_ref, zh_ref, zs_ref, ob_ref, act_ref, q_ref, k_ref, v_ref, *, tiles_per_seq):
    i = pl.program_id(0)

    @pl.when(i == 0)
    def _():
        q_ref[...] = jnp.zeros_like(q_ref)
        k_ref[...] = jnp.zeros_like(k_ref)
        v_ref[...] = jnp.zeros_like(v_ref)

    first_tile = (i - 1) % tiles_per_seq == 0
    attend = _attention_pieces(sink_ref, q_ref, k_ref, v_ref, ob_ref, first_tile)

    h = h_ref[...]
    g = g_ref[0]
    x = _rms(h, g[0:1, :]).astype(BF16)
    chunks = D_FF // FF_CHUNK
    per_chunk = -(-len(attend) // chunks)
    for c in range(chunks):
        mine = attend[c * per_chunk:(c + 1) * per_chunk]
        for scores, _ in mine:
            scores()
        gate = _dot(x, wgu_ref[0, :, c * FF_CHUNK:(c + 1) * FF_CHUNK])
        up = _dot(x, wgu_ref[0, :, D_FF + c * FF_CHUNK:D_FF + (c + 1) * FF_CHUNK])
        act_ref[:, c * FF_CHUNK:(c + 1) * FF_CHUNK] = (gate * jax.nn.sigmoid(gate) * up).astype(BF16)
        for _, values in mine:
            values()
    h = h + 0.5 * _rms(_dot(act_ref[...], wd_ref[0]), g[1:2, :])
    o_ref[...] = h

    z = _dot(_rms(h, g[2:3, :]).astype(BF16), win_ref[0])
    zh_ref[...] = z[:, COL_HG:COL_AQ]
    zs_ref[...] = z[:, COL_SU:]
    cos, sin = cos_ref[...], sin_ref[...]
    reps = ATT_WIDTH // LANES
    zq = z[:, COL_AQ:COL_AK]
    zk = z[:, COL_AK:COL_AV]
    k_ref[0:WINDOW, :] = k_ref[ATT_TILE:ATT_TILE + WINDOW, :]
    v_ref[0:WINDOW, :] = v_ref[ATT_TILE:ATT_TILE + WINDOW, :]
    q_ref[...] = (zq * jnp.concatenate([cos] * reps, axis=1)
                  + _swap_halves(zq) * jnp.concatenate([sin] * reps, axis=1)).astype(BF16)
    k_ref[WINDOW:, :] = (zk * cos + _swap_halves(zk) * sin).astype(BF16)
    v_ref[WINDOW:, :] = z[:, COL_AV:COL_SU].astype(BF16)


def _premix(h, gains, wgu, wd, w_in, sinks, cos, sin, layer, seq):
    n = h.shape[0]
    tm = ATT_TILE
    nt = n // tm
    row = lambda w: pl.BlockSpec((tm, w), lambda i: (jnp.minimum(i, nt - 1), 0))
    return pl.pallas_call(
        functools.partial(_premix_kernel, tiles_per_seq=seq // tm),
        out_shape=[jax.ShapeDtypeStruct((n, D_MODEL), F32), jax.ShapeDtypeStruct((n, COL_AQ), F32),
                   jax.ShapeDtypeStruct((n, IN_WIDTH - COL_SU), F32), jax.ShapeDtypeStruct((n, ATT_WIDTH), F32)],
        grid=(nt + 1,),
        in_specs=[pl.BlockSpec(memory_space=pltpu.SMEM), row(D_MODEL), _layer_spec(gains, layer),
                  _layer_spec(wgu, layer), _layer_spec(wd, layer), _layer_spec(w_in, layer), row(LANES), row(LANES)],
        out_specs=[row(D_MODEL), row(COL_AQ), row(IN_WIDTH - COL_SU),
                   pl.BlockSpec((tm, ATT_WIDTH), lambda i: (jnp.maximum(i - 1, 0), 0))],
        scratch_shapes=[pltpu.VMEM((tm, D_FF), BF16),
                        pltpu.VMEM((tm, ATT_WIDTH), BF16),
                        pltpu.VMEM((tm + WINDOW, ATT_KV_WIDTH), BF16),
                        pltpu.VMEM((tm + WINDOW, ATT_KV_WIDTH), BF16)],
        compiler_params=_params(1, "arbitrary"),
        name="premix",
    )(sinks, h, gains, wgu, wd, w_in, cos, sin)


def _postmix_kernel(h_ref, oa_ref, ob_ref, oc_ref, p_ref, g_ref, wout_ref, wgu_ref, wd_ref, wgate_ref, wproj_ref,
                    o_ref, act_ref):
    g = g_ref[0]
    y = _dot(oa_ref[...].astype(BF16), wout_ref[0, 0:HG_WIDTH, :])
    y = y + _dot(ob_ref[...].astype(BF16), wout_ref[0, HG_WIDTH:HG_WIDTH + ATT_WIDTH, :])
    y = y + _dot(oc_ref[...].astype(BF16), wout_ref[0, HG_WIDTH + ATT_WIDTH:, :])
    h = h_ref[...] + _rms(y, g[3:4, :])
    x = _rms(h, g[4:5, :]).astype(BF16)
    for c in range(D_FF // FF_CHUNK):
        gate = _dot(x, wgu_ref[0, :, c * FF_CHUNK:(c + 1) * FF_CHUNK])
        up = _dot(x, wgu_ref[0, :, D_FF + c * FF_CHUNK:D_FF + (c + 1) * FF_CHUNK])
        act_ref[:, c * FF_CHUNK:(c + 1) * FF_CHUNK] = (gate * jax.nn.sigmoid(gate) * up).astype(BF16)
    h = h + 0.5 * _rms(_dot(act_ref[...], wd_ref[0]), g[5:6, :])
    gate = jax.nn.sigmoid(_dot(_rms(h, g[6:7, :]).astype(BF16), wgate_ref[0]))
    e = _dot(p_ref[0].astype(BF16), wproj_ref[0])
    o_ref[...] = h + _rms(e * gate, g[7:8, :])


def _postmix(h, o_a, o_b, o_c, p, gains, w_out, wgu, wd, w_gate, w_proj, layer):
    n = h.shape[0]
    tm = min(TOKEN_TILE, n)
    row = lambda w: pl.BlockSpec((tm, w), lambda i: (i, 0))
    return pl.pallas_call(
        _postmix_kernel,
        out_shape=jax.ShapeDtypeStruct((n, D_MODEL), F32),
        grid=(n // tm,),
        in_specs=[row(D_MODEL), row(HG_WIDTH), row(ATT_WIDTH), row(SG_WIDTH),
                  pl.BlockSpec((1, tm, PLE_DIM), lambda i: (layer, i, 0)), _layer_spec(gains, layer),
                  _layer_spec(w_out, layer), _layer_spec(wgu, layer), _layer_spec(wd, layer),
                  _layer_spec(w_gate, layer), _layer_spec(w_proj, layer)],
        out_specs=row(D_MODEL),
        scratch_shapes=[pltpu.VMEM((tm, D_FF), BF16)],
        compiler_params=_params(1),
        name="postmix",
    )(h, o_a, o_b, o_c, p, gains, w_out, wgu, wd, w_gate, w_proj)


def _rope_table_kernel(pos_ref, inv_ref, sign_ref, cos_ref, sin_ref):
    ang = pos_ref[...] * inv_ref[...]
    cos_ref[...] = jnp.cos(ang)
    sin_ref[...] = jnp.sin(ang) * sign_ref[...]


def _rope_tables(positions):
    n = positions.size
    half = HEAD_DIM // 2
    inv = ROPE_THETA ** (-jnp.arange(half, dtype=F32) / half)
    inv = jnp.tile(inv, LANES // half)[None, :]
    sign = jnp.asarray(np.where((np.arange(LANES) % HEAD_DIM) < half, -1.0, 1.0)[None, :], F32)
    pos = jnp.broadcast_to(positions.astype(F32).reshape(n, 1), (n, LANES))
    tm = min(TOKEN_TILE, n)
    row = pl.BlockSpec((tm, LANES), lambda i: (i, 0))
    return pl.pallas_call(
        _rope_table_kernel,
        out_shape=[jax.ShapeDtypeStruct((n, LANES), F32)] * 2,
        grid=(n // tm,),
        in_specs=[row, _const_spec((1, LANES)), _const_spec((1, LANES))],
        out_specs=[row, row],
        compiler_params=_params(1),
        name="rope_tables",
    )(pos, inv, sign)


def _swap_halves(x):
    w = x.shape[-1]
    half = HEAD_DIM // 2
    lane = lax.broadcasted_iota(jnp.int32, x.shape, 1)
    first = (lane % HEAD_DIM) < half
    return jnp.where(first, pltpu.roll(x, w - half, 1), pltpu.roll(x, half, 1))


def _attention_pieces(sink_ref, q_ref, k_ref, v_ref, o_ref, first_tile):
    blocks = q_ref.shape[0] // WINDOW
    group = ATT_HEADS // ATT_KV_HEADS
    outs = [[None] * ATT_HEADS for _ in range(blocks)]
    order = [(hd, r) for r in range(blocks) for hd in range(ATT_HEADS)]

    def piece(hd, r, last):
        kv = hd // group
        lanes = slice(kv * HEAD_DIM, (kv + 1) * HEAD_DIM)
        keys = slice(r * WINDOW, (r + 2) * WINDOW)
        probs = []

        def scores():
            qi = lax.broadcasted_iota(jnp.int32, (WINDOW, 2 * WINDOW), 0)
            sj = lax.broadcasted_iota(jnp.int32, (WINDOW, 2 * WINDOW), 1)
            rel = qi + WINDOW - sj
            mask = (rel >= 0) & (rel < WINDOW)
            if r == 0:
                mask = mask & ((sj >= WINDOW) | jnp.logical_not(first_tile))
            sink = sink_ref[hd]
            qh = q_ref[r * WINDOW:(r + 1) * WINDOW, hd * HEAD_DIM:(hd + 1) * HEAD_DIM]
            s = _dot_nt(qh, k_ref[keys, lanes]) * (HEAD_DIM ** -0.5)
            s = jnp.where(mask, s, MASK_VALUE)
            m = jnp.maximum(jnp.max(s, axis=-1, keepdims=True), sink)
            pr = jnp.where(mask, jnp.exp(s - m), 0.0)
            denom = jnp.sum(pr, axis=-1, keepdims=True) + jnp.exp(sink - m)
            probs.append((pr * (1.0 / denom)).astype(BF16))

        def values():
            outs[r][hd] = _dot(probs[0], v_ref[keys, lanes])
            if last:
                for rr in range(blocks):
                    o_ref[rr * WINDOW:(rr + 1) * WINDOW, :] = jnp.concatenate(outs[rr], axis=1)

        return scores, values

    return [piece(hd, r, (hd, r) == order[-1]) for hd, r in order]


def _sgu_kernel(u_ref, v_ref, lng_ref, w_ref, bias_ref, o_ref):
    u = jax.nn.gelu(u_ref[...])
    v = jax.nn.gelu(v_ref[...])
    mu = jnp.mean(v, axis=-1, keepdims=True)
    vc = v - mu
    v = (vc * lax.rsqrt(jnp.mean(vc * vc, axis=-1, keepdims=True) + EPS) * lng_ref[...]).astype(BF16)
    ti = lax.broadcasted_iota(jnp.int32, (SG_CHUNK, SG_CHUNK), 0)
    si = lax.broadcasted_iota(jnp.int32, (SG_CHUNK, SG_CHUNK), 1)
    causal = si <= ti
    ws = [jnp.where(causal, w_ref[g], 0.0).astype(BF16) for g in range(SG_GROUPS)]
    lane_group = lax.broadcasted_iota(jnp.int32, (SG_CHUNK, SG_WIDTH), 1) // HEAD_DIM
    bias = bias_ref[...]
    for c in range(u.shape[0] // SG_CHUNK):
        rows = slice(c * SG_CHUNK, (c + 1) * SG_CHUNK)
        vch = v[rows]
        mix = bias
        for g in range(SG_GROUPS):
            mix = mix + jnp.where(lane_group == g, _dot(ws[g], vch), 0.0)
        o_ref[rows, :] = u[rows] * mix


def _sgu(z, ln_g, w_s, bias_tab):
    n = z.shape[0]
    tm = min(TOKEN_TILE, n)
    return pl.pallas_call(
        _sgu_kernel,
        out_shape=jax.ShapeDtypeStruct((n, SG_WIDTH), F32),
        grid=(n // tm,),
        in_specs=[pl.BlockSpec((tm, SG_WIDTH), lambda i: (i, 0)),
                  pl.BlockSpec((tm, SG_WIDTH), lambda i: (i, 1)),
                  _const_spec((1, SG_WIDTH)), _const_spec(w_s.shape), _const_spec(bias_tab.shape)],
        out_specs=pl.BlockSpec((tm, SG_WIDTH), lambda i: (i, 0)),
        compiler_params=_params(1),
        name="spatial_gating",
    )(z, z, ln_g, w_s, bias_tab)


def _group_cumsum(x):
    r = lax.broadcasted_iota(jnp.int32, x.shape, 0) % HG_SUB
    s = 1
    while s < HG_SUB:
        x = x + jnp.where(r >= s, pltpu.roll(x, s, 0), 0.0)
        s *= 2
    return x


def _group_last(x):
    n = x.shape[0]
    r = lax.broadcasted_iota(jnp.int32, x.shape, 0) % HG_SUB
    t = jnp.where(r == HG_SUB - 1, x, 0.0)
    s = 1
    while s < HG_SUB:
        t = t + pltpu.roll(t, n - s, 0)
        s *= 2
    return t


def _head_block_mask(shape):
    r = lax.broadcasted_iota(jnp.int32, shape, 0) // HEAD_DIM
    c = lax.broadcasted_iota(jnp.int32, shape, 1) // HEAD_DIM
    return r == c


def _dot_tn(a, b):
    return lax.dot_general(a, b, (((0,), (0,)), ((), ())), preferred_element_type=F32)


def _hgrn_kernel(z_ref, lbl_ref, gn_ref, o_ref, st_ref, qd_ref, kd_ref, vb_ref, t_ref, raw_ref, *, layer):
    @pl.when(pl.program_id(1) == 0)
    def _():
        st_ref[...] = jnp.zeros_like(st_ref)

    w = HG_WIDTH
    tile = z_ref.shape[0]
    groups = tile // HG_SUB
    half = HG_SUB // 2
    piece = groups * half
    q = z_ref[:, 0:w]
    x = z_ref[:, w:2 * w]
    v = z_ref[:, 2 * w:3 * w]

    lg = lbl_ref[...]
    e = jnp.exp(lg - jnp.max(lg, axis=0, keepdims=True))
    probs = e / jnp.sum(e, axis=0, keepdims=True)
    lb = jnp.sum(probs[0:layer + 1], axis=0, keepdims=True) - probs[0:1]
    lb_floor = jnp.maximum(lb, LB_FLOOR)

    t = jnp.exp(-jnp.abs(x))
    r = 1.0 / (1.0 + t)
    tr = t * r
    pos = x >= 0.0
    logf = jnp.log(lb_floor + (1.0 - lb) * jnp.where(pos, r, tr))
    k = (1.0 - lb) * jnp.where(pos, tr, r) + (lb - lb_floor)

    b = _group_cumsum(logf)
    b_last = _group_last(b)
    qd_ref[...] = (q * jnp.exp(b)).astype(BF16)
    kd_ref[...] = (k * jnp.exp(b_last - b)).astype(BF16)
    vb_ref[...] = v.astype(BF16)

    grp = lambda a: a.reshape(groups, HG_SUB, w)
    b3, q3, k3, v3 = grp(b * LOG2E), grp(q), grp(k), grp(v)
    b_lo, b_hi = b3[:, :half], b3[:, half:]
    q_lo, q_hi = q3[:, :half], q3[:, half:]
    row = lax.broadcasted_iota(jnp.int32, (groups, half, w), 1)
    pair = 2 * HEAD_DIM
    blockdiag = _head_block_mask((pair, pair))
    st = [st_ref[p] for p in range(w // pair)]

    def advance(m):
        rows = slice(m * HG_SUB, (m + 1) * HG_SUB)
        decay = jnp.exp(b_last[m * HG_SUB:m * HG_SUB + 1, :])
        for p in range(w // pair):
            lanes = slice(p * pair, (p + 1) * pair)
            raw_ref[rows, lanes] = _dot_nt(qd_ref[rows, lanes], st[p].astype(BF16))
            upd = _dot_tn(vb_ref[rows, lanes], kd_ref[rows, lanes])
            st[p] = st[p] * decay[:, lanes] + jnp.where(blockdiag, upd, 0.0)

    idx = 0
    for j in range(HG_SUB):
        for m in range(j * groups // HG_SUB, (j + 1) * groups // HG_SUB):
            advance(m)
        bj = b3[:, j:j + 1, :]
        kj = k3[:, j:j + 1, :]
        if j < half:
            dec = jnp.exp2(jnp.where(row >= j, b_lo - bj, MASK_VALUE))
            t_ref[idx * piece:(idx + 1) * piece, :] = (q_lo * kj * dec).reshape(piece, w).astype(BF16)
            idx += 1
            dec = jnp.exp2(b_hi - bj)
        else:
            dec = jnp.exp2(jnp.where(row >= j - half, b_hi - bj, MASK_VALUE))
        t_ref[idx * piece:(idx + 1) * piece, :] = (q_hi * kj * dec).reshape(piece, w).astype(BF16)
        idx += 1
    ones_bd = jnp.where(_head_block_mask((w, w)), 1.0, 0.0).astype(BF16)
    att = _dot(t_ref[...], ones_bd)
    o_lo = jnp.zeros((groups, half, w), F32)
    o_hi = jnp.zeros((groups, half, w), F32)
    idx = 0
    for j in range(HG_SUB):
        vj = v3[:, j:j + 1, :]
        if j < half:
            o_lo = o_lo + att[idx * piece:(idx + 1) * piece].reshape(groups, half, w) * vj
            idx += 1
        o_hi = o_hi + att[idx * piece:(idx + 1) * piece].reshape(groups, half, w) * vj
        idx += 1
    o_intra = jnp.concatenate([o_lo, o_hi], axis=1).reshape(tile, w)
    for p in range(w // pair):
        st_ref[p] = st[p]

    o = raw_ref[...] + o_intra
    sq = o * o
    hi = sq.astype(BF16)
    lo = (sq - hi.astype(F32)).astype(BF16)
    ms = (_dot(hi, ones_bd) + _dot(lo, ones_bd)) * (1.0 / HEAD_DIM)
    gate = jax.nn.sigmoid(z_ref[:, 3 * w:4 * w])
    o_ref[...] = o * lax.rsqrt(ms + EPS) * gn_ref[...] * gate


def _hgrn(z, lb_logits, norm_gain, layer, batch, seq):
    nt = seq // HG_TILE
    w = HG_WIDTH
    pair = 2 * HEAD_DIM
    pieces = HG_SUB + HG_SUB // 2
    return pl.pallas_call(
        functools.partial(_hgrn_kernel, layer=layer),
        out_shape=jax.ShapeDtypeStruct((batch * seq, w), F32),
        grid=(batch, nt),
        in_specs=[pl.BlockSpec((HG_TILE, 4 * w), lambda b, t: (b * nt + t, 0)),
                  _const_spec(lb_logits.shape), _const_spec((1, w))],
        out_specs=pl.BlockSpec((HG_TILE, w), lambda b, t: (b * nt + t, 0)),
        scratch_shapes=[pltpu.VMEM((w // pair, pair, pair), F32),
                        pltpu.VMEM((HG_TILE, w), BF16),
                        pltpu.VMEM((HG_TILE, w), BF16),
                        pltpu.VMEM((HG_TILE, w), BF16),
                        pltpu.VMEM((pieces * HG_TILE // 2, w), BF16),
                        pltpu.VMEM((HG_TILE, w), F32)],
        compiler_params=_params(2, "arbitrary"),
        name="hgrn2",
    )(z, lb_logits, norm_gain)


def kernel(x, p, positions, norm_gains, w_in, w_out, ffn1_gate_up, ffn1_down, ffn2_gate_up, ffn2_down,
           hgrn_lb_logits, hgrn_norm_gain, attn_sinks, sg_ln_gain, sg_spatial_w, sg_spatial_b, ple_proj, ple_gate):
    batch, seq, _ = x.shape
    depth = norm_gains.shape[0]
    n = batch * seq
    h = x.reshape(n, D_MODEL)
    cos, sin = _rope_tables(positions)

    bf = lambda w: w.astype(BF16)
    wgu1, wd1, win, wgu2, wd2 = bf(ffn1_gate_up), bf(ffn1_down), bf(w_in), bf(ffn2_gate_up), bf(ffn2_down)
    wout, wgate, wproj = bf(w_out), bf(ple_gate), bf(ple_proj)
    for l in range(depth):
        h, zh, zs, o_b = _premix(h, norm_gains, wgu1, wd1, win, attn_sinks[l], cos, sin, l, seq)
        o_a = _hgrn(zh, hgrn_lb_logits, jnp.tile(hgrn_norm_gain[l], HG_HEADS)[None, :], l, batch, seq)
        bias_tab = jnp.repeat(sg_spatial_b[l].T, HEAD_DIM, axis=1)
        o_c = _sgu(zs, sg_ln_gain[l][None, :], sg_spatial_w[l], bias_tab)
        h = _postmix(h, o_a, o_b, o_c, p.reshape(depth, n, PLE_DIM), norm_gains, wout, wgu2, wd2, wgate, wproj, l)
    return h.reshape(batch, seq, D_MODEL)
```

```python
import functools

import numpy as np
import jax
import jax.numpy as jnp
from jax import lax
from jax.experimental import pallas as pl
from jax.experimental.pallas import tpu as pltpu

F32 = jnp.float32
BF16 = jnp.bfloat16

D_MODEL = 1024
HEAD_DIM = 64
HG_HEADS = 4
HG_WIDTH = HG_HEADS * HEAD_DIM
ATT_HEADS = 8
ATT_KV_HEADS = 2
ATT_WIDTH = ATT_HEADS * HEAD_DIM
ATT_KV_WIDTH = ATT_KV_HEADS * HEAD_DIM
WINDOW = 128
ROPE_THETA = 10000.0
SG_GROUPS = 4
SG_WIDTH = SG_GROUPS * HEAD_DIM
SG_CHUNK = 128
IN_WIDTH = 4 * HG_WIDTH + ATT_WIDTH + 2 * ATT_KV_WIDTH + 2 * SG_WIDTH
D_FF = 2816
PLE_DIM = 256
EPS = 1e-6
MASK_VALUE = -1e30
LB_FLOOR = 1e-30
LOG2E = 1.4426950408889634

COL_HG = 0
COL_AQ = 4 * HG_WIDTH
COL_AK = COL_AQ + ATT_WIDTH
COL_AV = COL_AK + ATT_KV_WIDTH
COL_SU = COL_AV + ATT_KV_WIDTH
COL_SV = COL_SU + SG_WIDTH

LANES = 128
TOKEN_TILE = 512
FF_CHUNK = 256
ATT_TILE = 512
HG_SUB = 16
VMEM_LIMIT = 56 * 1024 * 1024


def _const_spec(shape):
    nd = len(shape)
    return pl.BlockSpec(shape, lambda *_: (0,) * nd, pipeline_mode=pl.Buffered(1))


def _rms(x, g):
    ms = jnp.mean(x * x, axis=-1, keepdims=True)
    return x * lax.rsqrt(ms + EPS) * g


def _dot(a, b):
    return jnp.dot(a, b, preferred_element_type=F32)


def _dot_nt(a, b):
    return lax.dot_general(a, b, (((1,), (1,)), ((), ())), preferred_element_type=F32)


def _params(n_axes, semantics="parallel"):
    return pltpu.CompilerParams(dimension_semantics=(semantics,) * n_axes, vmem_limit_bytes=VMEM_LIMIT)


def _layer_spec(a, layer):
    return pl.BlockSpec((1,) + a.shape[1:], lambda i: (layer,) + (0,) * (a.ndim - 1), pipeline_mode=pl.Buffered(1))


def _premix_kernel(sink_ref, h_ref, g_ref, wgu_ref, wd_ref, win_ref, cos_ref, sin_ref,
                   o_ref, zh_ref, zs_ref, ob_ref, act_ref, q_ref, k_ref, v_ref, *, tiles_per_seq):
    i = pl.program_id(0)

    @pl.when(i == 0)
    def _():
        q_ref[...] = jnp.zeros_like(q_ref)
        k_ref[...] = jnp.zeros_like(k_ref)
        v_ref[...] = jnp.zeros_like(v_ref)

    first_tile = (i - 1) % tiles_per_seq == 0
    attend = _attention_pieces(sink_ref, q_ref, k_ref, v_ref, ob_ref, first_tile)

    h = h_ref[...]
    g = g_ref[0]
    x = _rms(h, g[0:1, :]).astype(BF16)
    chunks = D_FF // FF_CHUNK
    per_chunk = -(-len(attend) // chunks)
    for c in range(chunks):
        mine = attend[c * per_chunk:(c + 1) * per_chunk]
        for scores, _ in mine:
            scores()
        gate = _dot(x, wgu_ref[0, :, c * FF_CHUNK:(c + 1) * FF_CHUNK])
        up = _dot(x, wgu_ref[0, :, D_FF + c * FF_CHUNK:D_FF + (c + 1) * FF_CHUNK])
        act_ref[:, c * FF_CHUNK:(c + 1) * FF_CHUNK] = (gate * jax.nn.sigmoid(gate) * up).astype(BF16)
        for _, values in mine:
            values()
    h = h + 0.5 * _rms(_dot(act_ref[...], wd_ref[0]), g[1:2, :])
    o_ref[...] = h

    z = _dot(_rms(h, g[2:3, :]).astype(BF16), win_ref[0])
    zh_ref[...] = z[:, COL_HG:COL_AQ]
    zs_ref[...] = z[:, COL_SU:]
    cos, sin = cos_ref[...], sin_ref[...]
    reps = ATT_WIDTH // LANES
    zq = z[:, COL_AQ:COL_AK]
    zk = z[:, COL_AK:COL_AV]
    k_ref[0:WINDOW, :] = k_ref[ATT_TILE:ATT_TILE + WINDOW, :]
    v_ref[0:WINDOW, :] = v_ref[ATT_TILE:ATT_TILE + WINDOW, :]
    q_ref[...] = (zq * jnp.concatenate([cos] * reps, axis=1)
                  + _swap_halves(zq) * jnp.concatenate([sin] * reps, axis=1)).astype(BF16)
    k_ref[WINDOW:, :] = (zk * cos + _swap_halves(zk) * sin).astype(BF16)
    v_ref[WINDOW:, :] = z[:, COL_AV:COL_SU].astype(BF16)


def _premix(h, gains, wgu, wd, w_in, sinks, cos, sin, layer, seq):
    n = h.shape[0]
    tm = ATT_TILE
    nt = n // tm
    row = lambda w: pl.BlockSpec((tm, w), lambda i: (jnp.minimum(i, nt - 1), 0))
    return pl.pallas_call(
        functools.partial(_premix_kernel, tiles_per_seq=seq // tm),
        out_shape=[jax.ShapeDtypeStruct((n, D_MODEL), F32), jax.ShapeDtypeStruct((n, COL_AQ), F32),
                   jax.ShapeDtypeStruct((n, IN_WIDTH - COL_SU), F32), jax.ShapeDtypeStruct((n, ATT_WIDTH), F32)],
        grid=(nt + 1,),
        in_specs=[pl.BlockSpec(memory_space=pltpu.SMEM), row(D_MODEL), _layer_spec(gains, layer),
                  _layer_spec(wgu, layer), _layer_spec(wd, layer), _layer_spec(w_in, layer), row(LANES), row(LANES)],
        out_specs=[row(D_MODEL), row(COL_AQ), row(IN_WIDTH - COL_SU),
                   pl.BlockSpec((tm, ATT_WIDTH), lambda i: (jnp.maximum(i - 1, 0), 0))],
        scratch_shapes=[pltpu.VMEM((tm, D_FF), BF16),
                        pltpu.VMEM((tm, ATT_WIDTH), BF16),
                        pltpu.VMEM((tm + WINDOW, ATT_KV_WIDTH), BF16),
                        pltpu.VMEM((tm + WINDOW, ATT_KV_WIDTH), BF16)],
        compiler_params=_params(1, "arbitrary"),
        name="premix",
    )(sinks, h, gains, wgu, wd, w_in, cos, sin)


def _postmix_kernel(zh_ref, lbl_ref, gn_ref, h_ref, ob_ref, oc_ref, p_ref, g_ref, wout_ref, wgu_ref, wd_ref,
                    wgate_ref, wproj_ref, o_ref, act_ref, oa_ref, oa_next_ref, *hgrn_refs, layer, tiles_per_seq):
    i = pl.program_id(0)

    @pl.when(i == 0)
    def _():
        oa_next_ref[...] = jnp.zeros_like(oa_next_ref)

    s = dict(zip([name for name, _ in _hgrn_scratch(zh_ref.shape[0])], hgrn_refs))

    @pl.when(i % tiles_per_seq == 0)
    def _():
        s["st"][...] = jnp.zeros_like(s["st"])

    oa_ref[...] = oa_next_ref[...]
    stages = _hgrn_stages(zh_ref, lbl_ref, gn_ref, oa_next_ref, s, layer)
    todo = list(stages)

    def hgrn(n):
        for _ in range(n):
            if todo:
                todo.pop(0)()

    g = g_ref[0]
    chunks = D_FF // FF_CHUNK
    per_chunk = -(-(len(stages) - 1) // (chunks + 1))
    hgrn(per_chunk)
    y = _dot(oa_ref[...].astype(BF16), wout_ref[0, 0:HG_WIDTH, :])
    y = y + _dot(ob_ref[...].astype(BF16), wout_ref[0, HG_WIDTH:HG_WIDTH + ATT_WIDTH, :])
    y = y + _dot(oc_ref[...].astype(BF16), wout_ref[0, HG_WIDTH + ATT_WIDTH:, :])
    h = h_ref[...] + _rms(y, g[3:4, :])
    x = _rms(h, g[4:5, :]).astype(BF16)
    for c in range(chunks):
        hgrn(min(per_chunk, len(todo) - 1))
        gate = _dot(x, wgu_ref[0, :, c * FF_CHUNK:(c + 1) * FF_CHUNK])
        up = _dot(x, wgu_ref[0, :, D_FF + c * FF_CHUNK:D_FF + (c + 1) * FF_CHUNK])
        act_ref[:, c * FF_CHUNK:(c + 1) * FF_CHUNK] = (gate * jax.nn.sigmoid(gate) * up).astype(BF16)
    hgrn(len(todo))
    h = h + 0.5 * _rms(_dot(act_ref[...], wd_ref[0]), g[5:6, :])
    gate = jax.nn.sigmoid(_dot(_rms(h, g[6:7, :]).astype(BF16), wgate_ref[0]))
    e = _dot(p_ref[0].astype(BF16), wproj_ref[0])
    o_ref[...] = h + _rms(e * gate, g[7:8, :])


def _postmix(h, zh, o_b, o_c, p, lb_logits, norm_gain, gains, w_out, wgu, wd, w_gate, w_proj, layer, seq):
    n = h.shape[0]
    tm = TOKEN_TILE
    nt = n // tm
    w = HG_WIDTH
    ahead = lambda width: pl.BlockSpec((tm, width), lambda i: (jnp.minimum(i, nt - 1), 0))
    behind = lambda width: pl.BlockSpec((tm, width), lambda i: (jnp.maximum(i - 1, 0), 0))
    return pl.pallas_call(
        functools.partial(_postmix_kernel, layer=layer, tiles_per_seq=seq // tm),
        out_shape=jax.ShapeDtypeStruct((n, D_MODEL), F32),
        grid=(nt + 1,),
        in_specs=[ahead(4 * w), _const_spec(lb_logits.shape), _const_spec((1, w)),
                  behind(D_MODEL), behind(ATT_WIDTH), behind(SG_WIDTH),
                  pl.BlockSpec((1, tm, PLE_DIM), lambda i: (layer, jnp.maximum(i - 1, 0), 0)), _layer_spec(gains, layer),
                  _layer_spec(w_out, layer), _layer_spec(wgu, layer), _layer_spec(wd, layer),
                  _layer_spec(w_gate, layer), _layer_spec(w_proj, layer)],
        out_specs=behind(D_MODEL),
        scratch_shapes=[pltpu.VMEM((tm, D_FF), BF16),
                        pltpu.VMEM((tm, w), F32),
                        pltpu.VMEM((tm, w), F32)]
                       + [spec for _, spec in _hgrn_scratch(tm)],
        compiler_params=_params(1, "arbitrary"),
        name="postmix",
    )(zh, lb_logits, norm_gain, h, o_b, o_c, p, gains, w_out, wgu, wd, w_gate, w_proj)


def _rope_table_kernel(pos_ref, inv_ref, cos_ref, sin_ref):
    ang = pos_ref[...] * inv_ref[...]
    cos_ref[...] = jnp.cos(ang)
    sin_ref[...] = jnp.sin(ang)


def _rope_tables(positions):
    n = positions.size
    half = HEAD_DIM // 2
    per_row = LANES // half
    inv = ROPE_THETA ** (-jnp.arange(half, dtype=F32) / half)
    pos = jnp.repeat(positions.astype(F32).reshape(n // per_row, per_row), half, axis=1)
    rows = n // per_row
    tm = min(TOKEN_TILE, rows)
    row = pl.BlockSpec((tm, LANES), lambda i: (i, 0))
    cos, sin = pl.pallas_call(
        _rope_table_kernel,
        out_shape=[jax.ShapeDtypeStruct((rows, LANES), F32)] * 2,
        grid=(rows // tm,),
        in_specs=[row, _const_spec((1, LANES))],
        out_specs=[row, row],
        compiler_params=_params(1),
        name="rope_tables",
    )(pos, jnp.tile(inv, per_row)[None, :])
    sign = jnp.asarray(np.where((np.arange(LANES) % HEAD_DIM) < half, -1.0, 1.0)[None, :], F32)
    spread = lambda t: jnp.tile(t.reshape(n, half), (1, per_row))
    return spread(cos), spread(sin) * sign


def _swap_halves(x):
    w = x.shape[-1]
    half = HEAD_DIM // 2
    lane = lax.broadcasted_iota(jnp.int32, x.shape, 1)
    first = (lane % HEAD_DIM) < half
    return jnp.where(first, pltpu.roll(x, w - half, 1), pltpu.roll(x, half, 1))


def _attention_pieces(sink_ref, q_ref, k_ref, v_ref, o_ref, first_tile):
    blocks = q_ref.shape[0] // WINDOW
    group = ATT_HEADS // ATT_KV_HEADS
    outs = [[None] * ATT_HEADS for _ in range(blocks)]
    order = [(hd, r) for r in range(blocks) for hd in range(ATT_HEADS)]

    def piece(hd, r, last):
        kv = hd // group
        lanes = slice(kv * HEAD_DIM, (kv + 1) * HEAD_DIM)
        keys = slice(r * WINDOW, (r + 2) * WINDOW)
        probs = []

        def scores():
            qi = lax.broadcasted_iota(jnp.int32, (WINDOW, 2 * WINDOW), 0)
            sj = lax.broadcasted_iota(jnp.int32, (WINDOW, 2 * WINDOW), 1)
            rel = qi + WINDOW - sj
            mask = (rel >= 0) & (rel < WINDOW)
            if r == 0:
                mask = mask & ((sj >= WINDOW) | jnp.logical_not(first_tile))
            sink = sink_ref[hd]
            qh = q_ref[r * WINDOW:(r + 1) * WINDOW, hd * HEAD_DIM:(hd + 1) * HEAD_DIM]
            s = _dot_nt(qh, k_ref[keys, lanes]) * (HEAD_DIM ** -0.5)
            s = jnp.where(mask, s, MASK_VALUE)
            m = jnp.maximum(jnp.max(s, axis=-1, keepdims=True), sink)
            pr = jnp.where(mask, jnp.exp(s - m), 0.0)
            denom = jnp.sum(pr, axis=-1, keepdims=True) + jnp.exp(sink - m)
            probs.append((pr * (1.0 / denom)).astype(BF16))

        def values():
            outs[r][hd] = _dot(probs[0], v_ref[keys, lanes])
            if last:
                for rr in range(blocks):
                    o_ref[rr * WINDOW:(rr + 1) * WINDOW, :] = jnp.concatenate(outs[rr], axis=1)

        return scores, values

    return [piece(hd, r, (hd, r) == order[-1]) for hd, r in order]


def _sgu_kernel(u_ref, v_ref, lng_ref, w_ref, bias_ref, o_ref):
    u = jax.nn.gelu(u_ref[...])
    v = jax.nn.gelu(v_ref[...])
    mu = jnp.mean(v, axis=-1, keepdims=True)
    vc = v - mu
    v = (vc * lax.rsqrt(jnp.mean(vc * vc, axis=-1, keepdims=True) + EPS) * lng_ref[...]).astype(BF16)
    ti = lax.broadcasted_iota(jnp.int32, (SG_CHUNK, SG_CHUNK), 0)
    si = lax.broadcasted_iota(jnp.int32, (SG_CHUNK, SG_CHUNK), 1)
    causal = si <= ti
    ws = [jnp.where(causal, w_ref[g], 0.0).astype(BF16) for g in range(SG_GROUPS)]
    lane_group = lax.broadcasted_iota(jnp.int32, (SG_CHUNK, SG_WIDTH), 1) // HEAD_DIM
    bias = bias_ref[...]
    for c in range(u.shape[0] // SG_CHUNK):
        rows = slice(c * SG_CHUNK, (c + 1) * SG_CHUNK)
        vch = v[rows]
        mix = bias
        for g in range(SG_GROUPS):
            mix = mix + jnp.where(lane_group == g, _dot(ws[g], vch), 0.0)
        o_ref[rows, :] = u[rows] * mix


def _sgu(z, ln_g, w_s, bias_tab):
    n = z.shape[0]
    tm = min(TOKEN_TILE, n)
    return pl.pallas_call(
        _sgu_kernel,
        out_shape=jax.ShapeDtypeStruct((n, SG_WIDTH), F32),
        grid=(n // tm,),
        in_specs=[pl.BlockSpec((tm, SG_WIDTH), lambda i: (i, 0)),
                  pl.BlockSpec((tm, SG_WIDTH), lambda i: (i, 1)),
                  _const_spec((1, SG_WIDTH)), _const_spec(w_s.shape), _const_spec(bias_tab.shape)],
        out_specs=pl.BlockSpec((tm, SG_WIDTH), lambda i: (i, 0)),
        compiler_params=_params(1),
        name="spatial_gating",
    )(z, z, ln_g, w_s, bias_tab)


def _group_cumsum(x):
    r = lax.broadcasted_iota(jnp.int32, x.shape, 0) % HG_SUB
    s = 1
    while s < HG_SUB:
        x = x + jnp.where(r >= s, pltpu.roll(x, s, 0), 0.0)
        s *= 2
    return x


def _group_last(x):
    n = x.shape[0]
    r = lax.broadcasted_iota(jnp.int32, x.shape, 0) % HG_SUB
    t = jnp.where(r == HG_SUB - 1, x, 0.0)
    s = 1
    while s < HG_SUB:
        t = t + pltpu.roll(t, n - s, 0)
        s *= 2
    return t


def _head_block_mask(shape):
    r = lax.broadcasted_iota(jnp.int32, shape, 0) // HEAD_DIM
    c = lax.broadcasted_iota(jnp.int32, shape, 1) // HEAD_DIM
    return r == c


def _dot_tn(a, b):
    return lax.dot_general(a, b, (((0,), (0,)), ((), ())), preferred_element_type=F32)


HG_BLOCK = 128
HG_STEPS = 8


def _hgrn_stages(z_ref, lbl_ref, gn_ref, o_ref, s, layer):
    w = HG_WIDTH
    tile = z_ref.shape[0]
    groups = tile // HG_SUB
    half = HG_SUB // 2
    piece = groups * half
    pair = 2 * HEAD_DIM
    pairs = w // pair

    def prepare(blk):
        def run():
            rows = slice(blk * HG_BLOCK, (blk + 1) * HG_BLOCK)
            q = z_ref[rows, 0:w]
            x = z_ref[rows, w:2 * w]
            v = z_ref[rows, 2 * w:3 * w]
            lg = lbl_ref[...]
            e = jnp.exp(lg - jnp.max(lg, axis=0, keepdims=True))
            probs = e / jnp.sum(e, axis=0, keepdims=True)
            lb = jnp.sum(probs[0:layer + 1], axis=0, keepdims=True) - probs[0:1]
            lb_floor = jnp.maximum(lb, LB_FLOOR)
            t = jnp.exp(-jnp.abs(x))
            r = 1.0 / (1.0 + t)
            tr = t * r
            pos = x >= 0.0
            logf = jnp.log(lb_floor + (1.0 - lb) * jnp.where(pos, r, tr))
            k = (1.0 - lb) * jnp.where(pos, tr, r) + (lb - lb_floor)
            b = _group_cumsum(logf)
            b_last = _group_last(b)
            s["qd"][rows, :] = (q * jnp.exp(b)).astype(BF16)
            s["kd"][rows, :] = (k * jnp.exp(b_last - b)).astype(BF16)
            s["vb"][rows, :] = v.astype(BF16)
            s["k"][rows, :] = k
            s["b2"][rows, :] = b * LOG2E
            per = HG_BLOCK // HG_SUB
            s["decay"][blk * per:(blk + 1) * per, :] = jnp.exp(b_last.reshape(per, HG_SUB, w)[:, 0, :])
            half_rows = slice(blk * HG_BLOCK // 2, (blk + 1) * HG_BLOCK // 2)
            s["o_lo"][half_rows, :] = jnp.zeros((HG_BLOCK // 2, w), F32)
            s["o_hi"][half_rows, :] = jnp.zeros((HG_BLOCK // 2, w), F32)
        return run

    def increments(gk):
        def run():
            blockdiag = _head_block_mask((pair, pair))
            for i in range(HG_STEPS):
                rows = slice((gk * HG_STEPS + i) * HG_SUB, (gk * HG_STEPS + i + 1) * HG_SUB)
                for p in range(pairs):
                    lanes = slice(p * pair, (p + 1) * pair)
                    upd = _dot_tn(s["vb"][rows, lanes], s["kd"][rows, lanes])
                    s["upd"][gk % 2, i, p] = jnp.where(blockdiag, upd, 0.0)
        return run

    def recur(gk):
        def run():
            st = [s["st"][p] for p in range(pairs)]
            for i in range(HG_STEPS):
                m = gk * HG_STEPS + i
                decay = s["decay"][m:m + 1, :]
                for p in range(pairs):
                    s["seen"][m, p] = st[p].astype(BF16)
                    st[p] = st[p] * decay[:, p * pair:(p + 1) * pair] + s["upd"][gk % 2, i, p]
            for p in range(pairs):
                s["st"][p] = st[p]
        return run

    def cross(gk):
        def run():
            for i in range(HG_STEPS):
                m = gk * HG_STEPS + i
                rows = slice(m * HG_SUB, (m + 1) * HG_SUB)
                for p in range(pairs):
                    lanes = slice(p * pair, (p + 1) * pair)
                    s["raw"][rows, lanes] = _dot_nt(s["qd"][rows, lanes], s["seen"][m, p])
        return run

    def key(j):
        def run():
            grp = lambda a: a.reshape(groups, HG_SUB, w)
            b3 = grp(s["b2"][...])
            q3 = grp(z_ref[:, 0:w])
            bj = b3[:, j:j + 1, :]
            kj = grp(s["k"][...])[:, j:j + 1, :]
            vj = grp(z_ref[:, 2 * w:3 * w])[:, j:j + 1, :]
            row = lax.broadcasted_iota(jnp.int32, (groups, half, w), 1)
            ones_bd = jnp.where(_head_block_mask((w, w)), 1.0, 0.0).astype(BF16)
            head_sums = lambda t: _dot(t.reshape(piece, w).astype(BF16), ones_bd).reshape(groups, half, w)
            if j < half:
                dec = jnp.exp2(jnp.where(row >= j, b3[:, :half] - bj, MASK_VALUE))
                s["o_lo"][...] += (head_sums(q3[:, :half] * kj * dec) * vj).reshape(piece, w)
                dec = jnp.exp2(b3[:, half:] - bj)
            else:
                dec = jnp.exp2(jnp.where(row >= j - half, b3[:, half:] - bj, MASK_VALUE))
            s["o_hi"][...] += (head_sums(q3[:, half:] * kj * dec) * vj).reshape(piece, w)
        return run

    def finish():
        ones_bd = jnp.where(_head_block_mask((w, w)), 1.0, 0.0).astype(BF16)
        intra = jnp.concatenate([s["o_lo"][...].reshape(groups, half, w), s["o_hi"][...].reshape(groups, half, w)],
                                axis=1).reshape(tile, w)
        o = s["raw"][...] + intra
        sq = o * o
        hi = sq.astype(BF16)
        lo = (sq - hi.astype(F32)).astype(BF16)
        ms = (_dot(hi, ones_bd) + _dot(lo, ones_bd)) * (1.0 / HEAD_DIM)
        gate = jax.nn.sigmoid(z_ref[:, 3 * w:4 * w])
        o_ref[...] = o * lax.rsqrt(ms + EPS) * gn_ref[...] * gate

    n_groups = groups // HG_STEPS
    keys = [key(j) for j in range(HG_SUB)]
    per_group = -(-HG_SUB // n_groups)
    stages = [prepare(blk) for blk in range(tile // HG_BLOCK)]
    for gk in range(n_groups + 2):
        if gk < n_groups:
            stages.append(increments(gk))
        if 1 <= gk <= n_groups:
            stages.append(recur(gk - 1))
        if 2 <= gk:
            stages.append(cross(gk - 2))
        stages += keys[:per_group]
        keys = keys[per_group:]
    return stages + keys + [finish]


def _hgrn_scratch(tile):
    w = HG_WIDTH
    pair = 2 * HEAD_DIM
    steps = tile // HG_SUB
    return [("st", pltpu.VMEM((w // pair, pair, pair), F32)),
            ("qd", pltpu.VMEM((tile, w), BF16)),
            ("kd", pltpu.VMEM((tile, w), BF16)),
            ("vb", pltpu.VMEM((tile, w), BF16)),
            ("k", pltpu.VMEM((tile, w), F32)),
            ("b2", pltpu.VMEM((tile, w), F32)),
            ("decay", pltpu.VMEM((steps, w), F32)),
            ("upd", pltpu.VMEM((2, HG_STEPS, w // pair, pair, pair), F32)),
            ("seen", pltpu.VMEM((steps, w // pair, pair, pair), BF16)),
            ("raw", pltpu.VMEM((tile, w), F32)),
            ("o_lo", pltpu.VMEM((tile // 2, w), F32)),
            ("o_hi", pltpu.VMEM((tile // 2, w), F32))]


def kernel(x, p, positions, norm_gains, w_in, w_out, ffn1_gate_up, ffn1_down, ffn2_gate_up, ffn2_down,
           hgrn_lb_logits, hgrn_norm_gain, attn_sinks, sg_ln_gain, sg_spatial_w, sg_spatial_b, ple_proj, ple_gate):
    batch, seq, _ = x.shape
    depth = norm_gains.shape[0]
    n = batch * seq
    h = x.reshape(n, D_MODEL)
    cos, sin = _rope_tables(positions)

    bf = lambda w: w.astype(BF16)
    wgu1, wd1, win, wgu2, wd2 = bf(ffn1_gate_up), bf(ffn1_down), bf(w_in), bf(ffn2_gate_up), bf(ffn2_down)
    wout, wgate, wproj = bf(w_out), bf(ple_gate), bf(ple_proj)
    for l in range(depth):
        h, zh, zs, o_b = _premix(h, norm_gains, wgu1, wd1, win, attn_sinks[l], cos, sin, l, seq)
        bias_tab = jnp.repeat(sg_spatial_b[l].T, HEAD_DIM, axis=1)
        o_c = _sgu(zs, sg_ln_gain[l][None, :], sg_spatial_w[l], bias_tab)
        h = _postmix(h, zh, o_b, o_c, p.reshape(depth, n, PLE_DIM), hgrn_lb_logits,
                     jnp.tile(hgrn_norm_gain[l], HG_HEADS)[None, :], norm_gains, wout, wgu2, wd2, wgate, wproj, l, seq)
    return h.reshape(batch, seq, D_MODEL)
```

```python
import functools

import numpy as np
import jax
import jax.numpy as jnp
from jax import lax
from jax.experimental import pallas as pl
from jax.experimental.pallas import tpu as pltpu

F32 = jnp.float32
BF16 = jnp.bfloat16

D_MODEL = 1024
HEAD_DIM = 64
HG_HEADS = 4
HG_WIDTH = HG_HEADS * HEAD_DIM
ATT_HEADS = 8
ATT_KV_HEADS = 2
ATT_WIDTH = ATT_HEADS * HEAD_DIM
ATT_KV_WIDTH = ATT_KV_HEADS * HEAD_DIM
WINDOW = 128
ROPE_THETA = 10000.0
SG_GROUPS = 4
SG_WIDTH = SG_GROUPS * HEAD_DIM
SG_CHUNK = 128
IN_WIDTH = 4 * HG_WIDTH + ATT_WIDTH + 2 * ATT_KV_WIDTH + 2 * SG_WIDTH
D_FF = 2816
PLE_DIM = 256
EPS = 1e-6
MASK_VALUE = -1e30
LB_FLOOR = 1e-30
LOG2E = 1.4426950408889634

COL_HG = 0
COL_AQ = 4 * HG_WIDTH
COL_AK = COL_AQ + ATT_WIDTH
COL_AV = COL_AK + ATT_KV_WIDTH
COL_SU = COL_AV + ATT_KV_WIDTH
COL_SV = COL_SU + SG_WIDTH

LANES = 128
TOKEN_TILE = 512
FF_CHUNK = 256
ROW_SPLIT = 2
ATT_TILE = 512
HG_SUB = 16
VMEM_LIMIT = 56 * 1024 * 1024


def _const_spec(shape):
    nd = len(shape)
    return pl.BlockSpec(shape, lambda *_: (0,) * nd, pipeline_mode=pl.Buffered(1))


def _rms(x, g):
    ms = jnp.mean(x * x, axis=-1, keepdims=True)
    return x * lax.rsqrt(ms + EPS) * g


def _dot(a, b):
    return jnp.dot(a, b, preferred_element_type=F32)


def _dot_nt(a, b):
    return lax.dot_general(a, b, (((1,), (1,)), ((), ())), preferred_element_type=F32)


def _params(n_axes, semantics="parallel"):
    return pltpu.CompilerParams(dimension_semantics=(semantics,) * n_axes, vmem_limit_bytes=VMEM_LIMIT)


def _row_blocks(rows):
    return [slice(b * rows // ROW_SPLIT, (b + 1) * rows // ROW_SPLIT) for b in range(ROW_SPLIT)]


def _layer_spec(a, layer):
    return pl.BlockSpec((1,) + a.shape[1:], lambda i: (layer,) + (0,) * (a.ndim - 1), pipeline_mode=pl.Buffered(1))


def _premix_kernel(sink_ref, h_ref, g_ref, wgu_ref, wd_ref, win_ref, cos_ref, sin_ref, lng_ref, ws_ref, bias_ref,
                   o_ref, zh_ref, ob_ref, oc_ref, act_ref, q_ref, k_ref, v_ref, su_ref, sv_ref, *, tiles_per_seq):
    i = pl.program_id(0)

    @pl.when(i == 0)
    def _():
        q_ref[...] = jnp.zeros_like(q_ref)
        k_ref[...] = jnp.zeros_like(k_ref)
        v_ref[...] = jnp.zeros_like(v_ref)
        su_ref[...] = jnp.zeros_like(su_ref)
        sv_ref[...] = jnp.zeros_like(sv_ref)

    first_tile = (i - 1) % tiles_per_seq == 0
    attend = _attention_pieces(sink_ref, q_ref, k_ref, v_ref, ob_ref, first_tile)
    gating = _sgu_pieces(su_ref, sv_ref, lng_ref, ws_ref, bias_ref, oc_ref)

    g = g_ref[0]
    x = _rms(h_ref[...], g[0:1, :]).astype(BF16)
    chunks = D_FF // FF_CHUNK
    per_chunk = -(-len(attend) // chunks)
    for c in range(chunks):
        mine = attend[c * per_chunk:(c + 1) * per_chunk]
        for scores, _ in mine:
            scores()
        gate = _dot(x, wgu_ref[0, :, c * FF_CHUNK:(c + 1) * FF_CHUNK])
        up = _dot(x, wgu_ref[0, :, D_FF + c * FF_CHUNK:D_FF + (c + 1) * FF_CHUNK])
        act_ref[:, c * FF_CHUNK:(c + 1) * FF_CHUNK] = (gate * jax.nn.sigmoid(gate) * up).astype(BF16)
        for _, values in mine:
            values()
        if c < len(gating):
            gating[c]()
    k_ref[0:WINDOW, :] = k_ref[ATT_TILE:ATT_TILE + WINDOW, :]
    v_ref[0:WINDOW, :] = v_ref[ATT_TILE:ATT_TILE + WINDOW, :]
    blocks = _row_blocks(h_ref.shape[0])
    down = [_dot(act_ref[r, :], wd_ref[0]) for r in blocks]
    reps = ATT_WIDTH // LANES
    for r, y in zip(blocks, down):
        hr = h_ref[r, :] + 0.5 * _rms(y, g[1:2, :])
        o_ref[r, :] = hr
        z = _dot(_rms(hr, g[2:3, :]).astype(BF16), win_ref[0])
        zh_ref[r, :] = z[:, COL_HG:COL_AQ]
        su_ref[r, :] = z[:, COL_SU:COL_SV]
        sv_ref[r, :] = z[:, COL_SV:]
        cos, sin = cos_ref[r, :], sin_ref[r, :]
        zq = z[:, COL_AQ:COL_AK]
        zk = z[:, COL_AK:COL_AV]
        kv_rows = slice(WINDOW + r.start, WINDOW + r.stop)
        q_ref[r, :] = (zq * jnp.concatenate([cos] * reps, axis=1)
                       + _swap_halves(zq) * jnp.concatenate([sin] * reps, axis=1)).astype(BF16)
        k_ref[kv_rows, :] = (zk * cos + _swap_halves(zk) * sin).astype(BF16)
        v_ref[kv_rows, :] = z[:, COL_AV:COL_SU].astype(BF16)


def _premix(h, gains, wgu, wd, w_in, sinks, cos, sin, sg_ln, sg_w, sg_bias, layer, seq):
    n = h.shape[0]
    tm = ATT_TILE
    nt = n // tm
    row = lambda w: pl.BlockSpec((tm, w), lambda i: (jnp.minimum(i, nt - 1), 0))
    behind = lambda w: pl.BlockSpec((tm, w), lambda i: (jnp.maximum(i - 1, 0), 0))
    return pl.pallas_call(
        functools.partial(_premix_kernel, tiles_per_seq=seq // tm),
        out_shape=[jax.ShapeDtypeStruct((n, D_MODEL), F32), jax.ShapeDtypeStruct((n, COL_AQ), F32),
                   jax.ShapeDtypeStruct((n, ATT_WIDTH), F32), jax.ShapeDtypeStruct((n, SG_WIDTH), F32)],
        grid=(nt + 1,),
        in_specs=[pl.BlockSpec(memory_space=pltpu.SMEM), row(D_MODEL), _layer_spec(gains, layer),
                  _layer_spec(wgu, layer), _layer_spec(wd, layer), _layer_spec(w_in, layer), row(LANES), row(LANES),
                  _const_spec(sg_ln.shape), _const_spec(sg_w.shape), _const_spec(sg_bias.shape)],
        out_specs=[row(D_MODEL), row(COL_AQ), behind(ATT_WIDTH), behind(SG_WIDTH)],
        scratch_shapes=[pltpu.VMEM((tm, D_FF), BF16),
                        pltpu.VMEM((tm, ATT_WIDTH), BF16),
                        pltpu.VMEM((tm + WINDOW, ATT_KV_WIDTH), BF16),
                        pltpu.VMEM((tm + WINDOW, ATT_KV_WIDTH), BF16),
                        pltpu.VMEM((tm, SG_WIDTH), F32),
                        pltpu.VMEM((tm, SG_WIDTH), F32)],
        compiler_params=_params(1, "arbitrary"),
        name="premix",
    )(sinks, h, gains, wgu, wd, w_in, cos, sin, sg_ln, sg_w, sg_bias)


def _postmix_kernel(zh_ref, lbl_ref, gn_ref, h_ref, ob_ref, oc_ref, p_ref, g_ref, wout_ref, wgu_ref, wd_ref,
                    wgate_ref, wproj_ref, o_ref, act_ref, x_ref, oa_ref, oa_next_ref, *hgrn_refs, layer, tiles_per_seq):
    i = pl.program_id(0)

    @pl.when(i == 0)
    def _():
        oa_next_ref[...] = jnp.zeros_like(oa_next_ref)

    s = dict(zip([name for name, _ in _hgrn_scratch(zh_ref.shape[0])], hgrn_refs))

    @pl.when(i % tiles_per_seq == 0)
    def _():
        s["st"][...] = jnp.zeros_like(s["st"])

    oa_ref[...] = oa_next_ref[...]
    chunks = D_FF // FF_CHUNK
    slots, finish = _hgrn_stages(zh_ref, lbl_ref, gn_ref, oa_next_ref, s, layer, chunks)

    g = g_ref[0]
    blocks = _row_blocks(h_ref.shape[0])
    mixed = [_dot(oa_ref[r, :].astype(BF16), wout_ref[0, 0:HG_WIDTH, :])
             + _dot(ob_ref[r, :].astype(BF16), wout_ref[0, HG_WIDTH:HG_WIDTH + ATT_WIDTH, :])
             + _dot(oc_ref[r, :].astype(BF16), wout_ref[0, HG_WIDTH + ATT_WIDTH:, :]) for r in blocks]
    for r, y in zip(blocks, mixed):
        hr = h_ref[r, :] + _rms(y, g[3:4, :])
        o_ref[r, :] = hr
        x_ref[r, :] = _rms(hr, g[4:5, :]).astype(BF16)
    for c in range(chunks):
        for stage in slots[c]:
            stage()
        gate = _dot(x_ref[...], wgu_ref[0, :, c * FF_CHUNK:(c + 1) * FF_CHUNK])
        up = _dot(x_ref[...], wgu_ref[0, :, D_FF + c * FF_CHUNK:D_FF + (c + 1) * FF_CHUNK])
        act_ref[:, c * FF_CHUNK:(c + 1) * FF_CHUNK] = (gate * jax.nn.sigmoid(gate) * up).astype(BF16)
    finish()
    down = [_dot(act_ref[r, :], wd_ref[0]) for r in blocks]
    gated = []
    for r, y in zip(blocks, down):
        hr = o_ref[r, :] + 0.5 * _rms(y, g[5:6, :])
        o_ref[r, :] = hr
        gate = jax.nn.sigmoid(_dot(_rms(hr, g[6:7, :]).astype(BF16), wgate_ref[0]))
        gated.append(_dot(p_ref[0, r, :].astype(BF16), wproj_ref[0]) * gate)
    for r, eg in zip(blocks, gated):
        o_ref[r, :] = o_ref[r, :] + _rms(eg, g[7:8, :])


def _postmix(h, zh, o_b, o_c, p, lb_logits, norm_gain, gains, w_out, wgu, wd, w_gate, w_proj, layer, seq):
    n = h.shape[0]
    tm = TOKEN_TILE
    nt = n // tm
    w = HG_WIDTH
    ahead = lambda width: pl.BlockSpec((tm, width), lambda i: (jnp.minimum(i, nt - 1), 0))
    behind = lambda width: pl.BlockSpec((tm, width), lambda i: (jnp.maximum(i - 1, 0), 0))
    return pl.pallas_call(
        functools.partial(_postmix_kernel, layer=layer, tiles_per_seq=seq // tm),
        out_shape=jax.ShapeDtypeStruct((n, D_MODEL), F32),
        grid=(nt + 1,),
        in_specs=[ahead(4 * w), _const_spec(lb_logits.shape), _const_spec((1, w)),
                  behind(D_MODEL), behind(ATT_WIDTH), behind(SG_WIDTH),
                  pl.BlockSpec((1, tm, PLE_DIM), lambda i: (layer, jnp.maximum(i - 1, 0), 0)), _layer_spec(gains, layer),
                  _layer_spec(w_out, layer), _layer_spec(wgu, layer), _layer_spec(wd, layer),
                  _layer_spec(w_gate, layer), _layer_spec(w_proj, layer)],
        out_specs=behind(D_MODEL),
        scratch_shapes=[pltpu.VMEM((tm, D_FF), BF16),
                        pltpu.VMEM((tm, D_MODEL), BF16),
                        pltpu.VMEM((tm, w), F32),
                        pltpu.VMEM((tm, w), F32)]
                       + [spec for _, spec in _hgrn_scratch(tm)],
        compiler_params=_params(1, "arbitrary"),
        name="postmix",
    )(zh, lb_logits, norm_gain, h, o_b, o_c, p, gains, w_out, wgu, wd, w_gate, w_proj)


def _rope_table_kernel(pos_ref, inv_ref, cos_ref, sin_ref):
    ang = pos_ref[...] * inv_ref[...]
    cos_ref[...] = jnp.cos(ang)
    sin_ref[...] = jnp.sin(ang)


def _rope_tables(positions):
    n = positions.size
    half = HEAD_DIM // 2
    per_row = LANES // half
    inv = ROPE_THETA ** (-jnp.arange(half, dtype=F32) / half)
    pos = jnp.repeat(positions.astype(F32).reshape(n // per_row, per_row), half, axis=1)
    rows = n // per_row
    tm = min(TOKEN_TILE, rows)
    row = pl.BlockSpec((tm, LANES), lambda i: (i, 0))
    cos, sin = pl.pallas_call(
        _rope_table_kernel,
        out_shape=[jax.ShapeDtypeStruct((rows, LANES), F32)] * 2,
        grid=(rows // tm,),
        in_specs=[row, _const_spec((1, LANES))],
        out_specs=[row, row],
        compiler_params=_params(1),
        name="rope_tables",
    )(pos, jnp.tile(inv, per_row)[None, :])
    sign = jnp.asarray(np.where((np.arange(LANES) % HEAD_DIM) < half, -1.0, 1.0)[None, :], F32)
    spread = lambda t: jnp.tile(t.reshape(n, half), (1, per_row))
    return spread(cos), spread(sin) * sign


def _swap_halves(x):
    w = x.shape[-1]
    half = HEAD_DIM // 2
    lane = lax.broadcasted_iota(jnp.int32, x.shape, 1)
    first = (lane % HEAD_DIM) < half
    return jnp.where(first, pltpu.roll(x, w - half, 1), pltpu.roll(x, half, 1))


def _attention_pieces(sink_ref, q_ref, k_ref, v_ref, o_ref, first_tile):
    blocks = q_ref.shape[0] // WINDOW
    group = ATT_HEADS // ATT_KV_HEADS
    outs = [[None] * ATT_HEADS for _ in range(blocks)]
    order = [(hd, r) for r in range(blocks) for hd in range(ATT_HEADS)]

    def piece(hd, r, last):
        kv = hd // group
        lanes = slice(kv * HEAD_DIM, (kv + 1) * HEAD_DIM)
        keys = slice(r * WINDOW, (r + 2) * WINDOW)
        probs = []

        def scores():
            qi = lax.broadcasted_iota(jnp.int32, (WINDOW, 2 * WINDOW), 0)
            sj = lax.broadcasted_iota(jnp.int32, (WINDOW, 2 * WINDOW), 1)
            rel = qi + WINDOW - sj
            mask = (rel >= 0) & (rel < WINDOW)
            if r == 0:
                mask = mask & ((sj >= WINDOW) | jnp.logical_not(first_tile))
            sink = sink_ref[hd]
            qh = q_ref[r * WINDOW:(r + 1) * WINDOW, hd * HEAD_DIM:(hd + 1) * HEAD_DIM]
            s = _dot_nt(qh, k_ref[keys, lanes]) * (HEAD_DIM ** -0.5)
            s = jnp.where(mask, s, MASK_VALUE)
            m = jnp.maximum(jnp.max(s, axis=-1, keepdims=True), sink)
            pr = jnp.where(mask, jnp.exp(s - m), 0.0)
            denom = jnp.sum(pr, axis=-1, keepdims=True) + jnp.exp(sink - m)
            probs.append((pr * (1.0 / denom)).astype(BF16))

        def values():
            outs[r][hd] = _dot(probs[0], v_ref[keys, lanes])
            if last:
                for rr in range(blocks):
                    o_ref[rr * WINDOW:(rr + 1) * WINDOW, :] = jnp.concatenate(outs[rr], axis=1)

        return scores, values

    return [piece(hd, r, (hd, r) == order[-1]) for hd, r in order]


def _sgu_pieces(u_ref, v_ref, lng_ref, w_ref, bias_ref, o_ref):
    def piece(c):
        def run():
            rows = slice(c * SG_CHUNK, (c + 1) * SG_CHUNK)
            u = jax.nn.gelu(u_ref[rows, :])
            v = jax.nn.gelu(v_ref[rows, :])
            mu = jnp.mean(v, axis=-1, keepdims=True)
            vc = v - mu
            v = (vc * lax.rsqrt(jnp.mean(vc * vc, axis=-1, keepdims=True) + EPS) * lng_ref[...]).astype(BF16)
            ti = lax.broadcasted_iota(jnp.int32, (SG_CHUNK, SG_CHUNK), 0)
            si = lax.broadcasted_iota(jnp.int32, (SG_CHUNK, SG_CHUNK), 1)
            lane_group = lax.broadcasted_iota(jnp.int32, (SG_CHUNK, SG_WIDTH), 1) // HEAD_DIM
            mix = bias_ref[...]
            for g in range(SG_GROUPS):
                wg = jnp.where(si <= ti, w_ref[g], 0.0).astype(BF16)
                mix = mix + jnp.where(lane_group == g, _dot(wg, v), 0.0)
            o_ref[rows, :] = u * mix
        return run

    return [piece(c) for c in range(u_ref.shape[0] // SG_CHUNK)]


def _group_cumsum(x):
    r = lax.broadcasted_iota(jnp.int32, x.shape, 0) % HG_SUB
    s = 1
    while s < HG_SUB:
        x = x + jnp.where(r >= s, pltpu.roll(x, s, 0), 0.0)
        s *= 2
    return x


def _group_last(x):
    n = x.shape[0]
    r = lax.broadcasted_iota(jnp.int32, x.shape, 0) % HG_SUB
    t = jnp.where(r == HG_SUB - 1, x, 0.0)
    s = 1
    while s < HG_SUB:
        t = t + pltpu.roll(t, n - s, 0)
        s *= 2
    return t


def _head_block_mask(shape):
    r = lax.broadcasted_iota(jnp.int32, shape, 0) // HEAD_DIM
    c = lax.broadcasted_iota(jnp.int32, shape, 1) // HEAD_DIM
    return r == c


def _dot_tn(a, b):
    return lax.dot_general(a, b, (((0,), (0,)), ((), ())), preferred_element_type=F32)


HG_BLOCK = 128
HG_STEPS = 8


def _hgrn_stages(z_ref, lbl_ref, gn_ref, o_ref, s, layer, n_slots):
    w = HG_WIDTH
    tile = z_ref.shape[0]
    groups = tile // HG_SUB
    half = HG_SUB // 2
    piece = groups * half
    pair = 2 * HEAD_DIM
    pairs = w // pair

    def prepare(blk):
        def run():
            rows = slice(blk * HG_BLOCK, (blk + 1) * HG_BLOCK)
            q = z_ref[rows, 0:w]
            x = z_ref[rows, w:2 * w]
            v = z_ref[rows, 2 * w:3 * w]
            lg = lbl_ref[...]
            e = jnp.exp(lg - jnp.max(lg, axis=0, keepdims=True))
            probs = e / jnp.sum(e, axis=0, keepdims=True)
            lb = jnp.sum(probs[0:layer + 1], axis=0, keepdims=True) - probs[0:1]
            lb_floor = jnp.maximum(lb, LB_FLOOR)
            t = jnp.exp(-jnp.abs(x))
            r = 1.0 / (1.0 + t)
            tr = t * r
            pos = x >= 0.0
            logf = jnp.log(lb_floor + (1.0 - lb) * jnp.where(pos, r, tr))
            k = (1.0 - lb) * jnp.where(pos, tr, r) + (lb - lb_floor)
            b = _group_cumsum(logf)
            b_last = _group_last(b)
            s["qd"][rows, :] = (q * jnp.exp(b)).astype(BF16)
            s["kd"][rows, :] = (k * jnp.exp(b_last - b)).astype(BF16)
            s["vb"][rows, :] = v.astype(BF16)
            s["k"][rows, :] = k
            s["b2"][rows, :] = b * LOG2E
            per = HG_BLOCK // HG_SUB
            s["decay"][blk * per:(blk + 1) * per, :] = jnp.exp(b_last.reshape(per, HG_SUB, w)[:, 0, :])
            half_rows = slice(blk * HG_BLOCK // 2, (blk + 1) * HG_BLOCK // 2)
            s["o_lo"][half_rows, :] = jnp.zeros((HG_BLOCK // 2, w), F32)
            s["o_hi"][half_rows, :] = jnp.zeros((HG_BLOCK // 2, w), F32)
        return run

    def increments(gk):
        def run():
            blockdiag = _head_block_mask((pair, pair))
            for i in range(HG_STEPS):
                rows = slice((gk * HG_STEPS + i) * HG_SUB, (gk * HG_STEPS + i + 1) * HG_SUB)
                for p in range(pairs):
                    lanes = slice(p * pair, (p + 1) * pair)
                    upd = _dot_tn(s["vb"][rows, lanes], s["kd"][rows, lanes])
                    s["upd"][gk % 2, i, p] = jnp.where(blockdiag, upd, 0.0)
        return run

    def recur(gk):
        def run():
            st = [s["st"][p] for p in range(pairs)]
            for i in range(HG_STEPS):
                m = gk * HG_STEPS + i
                decay = s["decay"][m:m + 1, :]
                for p in range(pairs):
                    s["seen"][m, p] = st[p].astype(BF16)
                    st[p] = st[p] * decay[:, p * pair:(p + 1) * pair] + s["upd"][gk % 2, i, p]
            for p in range(pairs):
                s["st"][p] = st[p]
        return run

    def cross(gk):
        def run():
            for i in range(HG_STEPS):
                m = gk * HG_STEPS + i
                rows = slice(m * HG_SUB, (m + 1) * HG_SUB)
                for p in range(pairs):
                    lanes = slice(p * pair, (p + 1) * pair)
                    s["raw"][rows, lanes] = _dot_nt(s["qd"][rows, lanes], s["seen"][m, p])
        return run

    def key(j):
        def run():
            grp = lambda a: a.reshape(groups, HG_SUB, w)
            b3 = grp(s["b2"][...])
            q3 = grp(z_ref[:, 0:w])
            bj = b3[:, j:j + 1, :]
            kj = grp(s["k"][...])[:, j:j + 1, :]
            vj = grp(z_ref[:, 2 * w:3 * w])[:, j:j + 1, :]
            row = lax.broadcasted_iota(jnp.int32, (groups, half, w), 1)
            ones_bd = jnp.where(_head_block_mask((w, w)), 1.0, 0.0).astype(BF16)
            head_sums = lambda t: _dot(t.reshape(piece, w).astype(BF16), ones_bd).reshape(groups, half, w)
            if j < half:
                dec = jnp.exp2(jnp.where(row >= j, b3[:, :half] - bj, MASK_VALUE))
                s["o_lo"][...] += (head_sums(q3[:, :half] * kj * dec) * vj).reshape(piece, w)
                dec = jnp.exp2(b3[:, half:] - bj)
            else:
                dec = jnp.exp2(jnp.where(row >= j - half, b3[:, half:] - bj, MASK_VALUE))
            s["o_hi"][...] += (head_sums(q3[:, half:] * kj * dec) * vj).reshape(piece, w)
        return run

    def finish():
        ones_bd = jnp.where(_head_block_mask((w, w)), 1.0, 0.0).astype(BF16)
        intra = jnp.concatenate([s["o_lo"][...].reshape(groups, half, w), s["o_hi"][...].reshape(groups, half, w)],
                                axis=1).reshape(tile, w)
        o = s["raw"][...] + intra
        sq = o * o
        hi = sq.astype(BF16)
        lo = (sq - hi.astype(F32)).astype(BF16)
        ms = (_dot(hi, ones_bd) + _dot(lo, ones_bd)) * (1.0 / HEAD_DIM)
        gate = jax.nn.sigmoid(z_ref[:, 3 * w:4 * w])
        o_ref[...] = o * lax.rsqrt(ms + EPS) * gn_ref[...] * gate

    n_groups = groups // HG_STEPS
    assert tile // HG_BLOCK == n_groups and n_groups < n_slots
    slots = [[] for _ in range(n_slots)]
    for t in range(n_groups + 3):
        slot = slots[min(t, n_slots - 1)]
        if t < n_groups:
            slot.append(prepare(t))
        if 0 <= t - 1 < n_groups:
            slot.append(increments(t - 1))
        if 0 <= t - 2 < n_groups:
            slot.append(recur(t - 2))
        if 0 <= t - 3 < n_groups:
            slot.append(cross(t - 3))
    key_slots = n_slots - n_groups
    for j in range(HG_SUB):
        slots[n_groups + j * key_slots // HG_SUB].append(key(j))
    return slots, finish


def _hgrn_scratch(tile):
    w = HG_WIDTH
    pair = 2 * HEAD_DIM
    steps = tile // HG_SUB
    return [("st", pltpu.VMEM((w // pair, pair, pair), F32)),
            ("qd", pltpu.VMEM((tile, w), BF16)),
            ("kd", pltpu.VMEM((tile, w), BF16)),
            ("vb", pltpu.VMEM((tile, w), BF16)),
            ("k", pltpu.VMEM((tile, w), F32)),
            ("b2", pltpu.VMEM((tile, w), F32)),
            ("decay", pltpu.VMEM((steps, w), F32)),
            ("upd", pltpu.VMEM((2, HG_STEPS, w // pair, pair, pair), F32)),
            ("seen", pltpu.VMEM((steps, w // pair, pair, pair), BF16)),
            ("raw", pltpu.VMEM((tile, w), F32)),
            ("o_lo", pltpu.VMEM((tile // 2, w), F32)),
            ("o_hi", pltpu.VMEM((tile // 2, w), F32))]


def kernel(x, p, positions, norm_gains, w_in, w_out, ffn1_gate_up, ffn1_down, ffn2_gate_up, ffn2_down,
           hgrn_lb_logits, hgrn_norm_gain, attn_sinks, sg_ln_gain, sg_spatial_w, sg_spatial_b, ple_proj, ple_gate):
    batch, seq, _ = x.shape
    depth = norm_gains.shape[0]
    n = batch * seq
    h = x.reshape(n, D_MODEL)
    cos, sin = _rope_tables(positions)

    bf = lambda w: w.astype(BF16)
    wgu1, wd1, win, wgu2, wd2 = bf(ffn1_gate_up), bf(ffn1_down), bf(w_in), bf(ffn2_gate_up), bf(ffn2_down)
    wout, wgate, wproj = bf(w_out), bf(ple_gate), bf(ple_proj)
    for l in range(depth):
        bias_tab = jnp.repeat(sg_spatial_b[l].T, HEAD_DIM, axis=1)
        h, zh, o_b, o_c = _premix(h, norm_gains, wgu1, wd1, win, attn_sinks[l], cos, sin, sg_ln_gain[l][None, :],
                                  sg_spatial_w[l], bias_tab, l, seq)
        h = _postmix(h, zh, o_b, o_c, p.reshape(depth, n, PLE_DIM), hgrn_lb_logits,
                     jnp.tile(hgrn_norm_gain[l], HG_HEADS)[None, :], norm_gains, wout, wgu2, wd2, wgate, wproj, l, seq)
    return h.reshape(batch, seq, D_MODEL)
```

```python
import functools

import numpy as np
import jax
import jax.numpy as jnp
from jax import lax
from jax.experimental import pallas as pl
from jax.experimental.pallas import tpu as pltpu

F32 = jnp.float32
BF16 = jnp.bfloat16

D_MODEL = 1024
HEAD_DIM = 64
HG_HEADS = 4
HG_WIDTH = HG_HEADS * HEAD_DIM
ATT_HEADS = 8
ATT_KV_HEADS = 2
ATT_WIDTH = ATT_HEADS * HEAD_DIM
ATT_KV_WIDTH = ATT_KV_HEADS * HEAD_DIM
WINDOW = 128
ROPE_THETA = 10000.0
SG_GROUPS = 4
SG_WIDTH = SG_GROUPS * HEAD_DIM
SG_CHUNK = 128
IN_WIDTH = 4 * HG_WIDTH + ATT_WIDTH + 2 * ATT_KV_WIDTH + 2 * SG_WIDTH
D_FF = 2816
PLE_DIM = 256
EPS = 1e-6
MASK_VALUE = -1e30
LB_FLOOR = 1e-30
LOG2E = 1.4426950408889634

COL_HG = 0
COL_AQ = 4 * HG_WIDTH
COL_AK = COL_AQ + ATT_WIDTH
COL_AV = COL_AK + ATT_KV_WIDTH
COL_SU = COL_AV + ATT_KV_WIDTH
COL_SV = COL_SU + SG_WIDTH

LANES = 128
TOKEN_TILE = 512
FF_CHUNK = 256
ROW_SPLIT = 2
ATT_TILE = 512
HG_SUB = 16
VMEM_LIMIT = 56 * 1024 * 1024


def _const_spec(shape):
    nd = len(shape)
    return pl.BlockSpec(shape, lambda *_: (0,) * nd, pipeline_mode=pl.Buffered(1))


def _rms(x, g):
    ms = jnp.mean(x * x, axis=-1, keepdims=True)
    return x * lax.rsqrt(ms + EPS) * g


def _dot(a, b):
    return jnp.dot(a, b, preferred_element_type=F32)


def _dot_nt(a, b):
    return lax.dot_general(a, b, (((1,), (1,)), ((), ())), preferred_element_type=F32)


def _params(n_axes, semantics="parallel"):
    return pltpu.CompilerParams(dimension_semantics=(semantics,) * n_axes, vmem_limit_bytes=VMEM_LIMIT)


def _row_blocks(rows):
    return [slice(b * rows // ROW_SPLIT, (b + 1) * rows // ROW_SPLIT) for b in range(ROW_SPLIT)]


def _layer_spec(a, layer):
    return pl.BlockSpec((1,) + a.shape[1:], lambda i: (layer,) + (0,) * (a.ndim - 1), pipeline_mode=pl.Buffered(1))


def _premix_kernel(sink_ref, h_ref, g_ref, wgu_ref, wd_ref, win_ref, cos_ref, sin_ref, lng_ref, ws_ref, bias_ref,
                   o_ref, zh_ref, ob_ref, oc_ref, act_ref, q_ref, k_ref, v_ref, su_ref, sv_ref, *, tiles_per_seq, n_tiles):
    i = pl.program_id(0)

    @pl.when(i == 0)
    def _():
        q_ref[...] = jnp.zeros_like(q_ref)
        k_ref[...] = jnp.zeros_like(k_ref)
        v_ref[...] = jnp.zeros_like(v_ref)
        su_ref[...] = jnp.zeros_like(su_ref)
        sv_ref[...] = jnp.zeros_like(sv_ref)

    _premix_step(True, sink_ref, h_ref, g_ref, wgu_ref, wd_ref, win_ref, cos_ref, sin_ref, lng_ref, ws_ref, bias_ref,
                 o_ref, zh_ref, ob_ref, oc_ref, act_ref, q_ref, k_ref, v_ref, su_ref, sv_ref, tiles_per_seq)


def _premix_step(token_work, sink_ref, h_ref, g_ref, wgu_ref, wd_ref, win_ref, cos_ref, sin_ref, lng_ref, ws_ref, bias_ref,
                 o_ref, zh_ref, ob_ref, oc_ref, act_ref, q_ref, k_ref, v_ref, su_ref, sv_ref, tiles_per_seq):
    i = pl.program_id(0)
    first_tile = (i - 1) % tiles_per_seq == 0
    attend = _attention_pieces(sink_ref, q_ref, k_ref, v_ref, ob_ref, first_tile)
    gating = _sgu_pieces(su_ref, sv_ref, lng_ref, ws_ref, bias_ref, oc_ref)
    if not token_work:
        for scores, values in attend:
            scores()
            values()
        for piece in gating:
            piece()
        return

    g = g_ref[0]
    x = _rms(h_ref[...], g[0:1, :]).astype(BF16)
    chunks = D_FF // FF_CHUNK
    per_chunk = -(-len(attend) // chunks)
    for c in range(chunks):
        mine = attend[c * per_chunk:(c + 1) * per_chunk]
        for scores, _ in mine:
            scores()
        gate = _dot(x, wgu_ref[0, :, c * FF_CHUNK:(c + 1) * FF_CHUNK])
        up = _dot(x, wgu_ref[0, :, D_FF + c * FF_CHUNK:D_FF + (c + 1) * FF_CHUNK])
        act_ref[:, c * FF_CHUNK:(c + 1) * FF_CHUNK] = (gate * jax.nn.sigmoid(gate) * up).astype(BF16)
        for _, values in mine:
            values()
        if c < len(gating):
            gating[c]()
    k_ref[0:WINDOW, :] = k_ref[ATT_TILE:ATT_TILE + WINDOW, :]
    v_ref[0:WINDOW, :] = v_ref[ATT_TILE:ATT_TILE + WINDOW, :]
    blocks = _row_blocks(h_ref.shape[0])
    down = [_dot(act_ref[r, :], wd_ref[0]) for r in blocks]
    reps = ATT_WIDTH // LANES
    for r, y in zip(blocks, down):
        hr = h_ref[r, :] + 0.5 * _rms(y, g[1:2, :])
        o_ref[r, :] = hr
        z = _dot(_rms(hr, g[2:3, :]).astype(BF16), win_ref[0])
        zh_ref[r, :] = z[:, COL_HG:COL_AQ]
        su_ref[r, :] = z[:, COL_SU:COL_SV]
        sv_ref[r, :] = z[:, COL_SV:]
        cos, sin = cos_ref[r, :], sin_ref[r, :]
        zq = z[:, COL_AQ:COL_AK]
        zk = z[:, COL_AK:COL_AV]
        kv_rows = slice(WINDOW + r.start, WINDOW + r.stop)
        q_ref[r, :] = (zq * jnp.concatenate([cos] * reps, axis=1)
                       + _swap_halves(zq) * jnp.concatenate([sin] * reps, axis=1)).astype(BF16)
        k_ref[kv_rows, :] = (zk * cos + _swap_halves(zk) * sin).astype(BF16)
        v_ref[kv_rows, :] = z[:, COL_AV:COL_SU].astype(BF16)


def _premix(h, gains, wgu, wd, w_in, sinks, cos, sin, sg_ln, sg_w, sg_bias, layer, seq):
    n = h.shape[0]
    tm = ATT_TILE
    nt = n // tm
    row = lambda w: pl.BlockSpec((tm, w), lambda i: (jnp.minimum(i, nt - 1), 0))
    behind = lambda w: pl.BlockSpec((tm, w), lambda i: (jnp.maximum(i - 1, 0), 0))
    return pl.pallas_call(
        functools.partial(_premix_kernel, tiles_per_seq=seq // tm, n_tiles=nt),
        out_shape=[jax.ShapeDtypeStruct((n, D_MODEL), F32), jax.ShapeDtypeStruct((n, COL_AQ), F32),
                   jax.ShapeDtypeStruct((n, ATT_WIDTH), F32), jax.ShapeDtypeStruct((n, SG_WIDTH), F32)],
        grid=(nt + 1,),
        in_specs=[pl.BlockSpec(memory_space=pltpu.SMEM), row(D_MODEL), _layer_spec(gains, layer),
                  _layer_spec(wgu, layer), _layer_spec(wd, layer), _layer_spec(w_in, layer), row(LANES), row(LANES),
                  _const_spec(sg_ln.shape), _const_spec(sg_w.shape), _const_spec(sg_bias.shape)],
        out_specs=[row(D_MODEL), row(COL_AQ), behind(ATT_WIDTH), behind(SG_WIDTH)],
        scratch_shapes=[pltpu.VMEM((tm, D_FF), BF16),
                        pltpu.VMEM((tm, ATT_WIDTH), BF16),
                        pltpu.VMEM((tm + WINDOW, ATT_KV_WIDTH), BF16),
                        pltpu.VMEM((tm + WINDOW, ATT_KV_WIDTH), BF16),
                        pltpu.VMEM((tm, SG_WIDTH), F32),
                        pltpu.VMEM((tm, SG_WIDTH), F32)],
        compiler_params=_params(1, "arbitrary"),
        name="premix",
    )(sinks, h, gains, wgu, wd, w_in, cos, sin, sg_ln, sg_w, sg_bias)


def _postmix_kernel(zh_ref, lbl_ref, gn_ref, h_ref, ob_ref, oc_ref, p_ref, g_ref, wout_ref, wgu_ref, wd_ref,
                    wgate_ref, wproj_ref, o_ref, act_ref, x_ref, oa_ref, oa_next_ref, *hgrn_refs, layer, tiles_per_seq):
    i = pl.program_id(0)
    s = dict(zip([name for name, _ in _hgrn_scratch(zh_ref.shape[0])], hgrn_refs))

    @pl.when(i % tiles_per_seq == 0)
    def _():
        s["st"][...] = jnp.zeros_like(s["st"])

    @pl.when(i == 0)
    def _():
        oa_next_ref[...] = jnp.zeros_like(oa_next_ref)

    _postmix_step(True, zh_ref, lbl_ref, gn_ref, h_ref, ob_ref, oc_ref, p_ref, g_ref, wout_ref, wgu_ref, wd_ref,
                  wgate_ref, wproj_ref, o_ref, act_ref, x_ref, oa_ref, oa_next_ref, s, layer)


def _postmix_step(token_work, zh_ref, lbl_ref, gn_ref, h_ref, ob_ref, oc_ref, p_ref, g_ref, wout_ref, wgu_ref, wd_ref,
                  wgate_ref, wproj_ref, o_ref, act_ref, x_ref, oa_ref, oa_next_ref, s, layer):
    chunks = D_FF // FF_CHUNK
    if not token_work:
        slots, finish = _hgrn_stages(zh_ref, lbl_ref, gn_ref, oa_next_ref, s, layer, chunks)
        for slot in slots:
            for stage in slot:
                stage()
        finish()
        return

    oa_ref[...] = oa_next_ref[...]
    slots, finish = _hgrn_stages(zh_ref, lbl_ref, gn_ref, oa_next_ref, s, layer, chunks)

    g = g_ref[0]
    blocks = _row_blocks(h_ref.shape[0])
    mixed = [_dot(oa_ref[r, :].astype(BF16), wout_ref[0, 0:HG_WIDTH, :])
             + _dot(ob_ref[r, :].astype(BF16), wout_ref[0, HG_WIDTH:HG_WIDTH + ATT_WIDTH, :])
             + _dot(oc_ref[r, :].astype(BF16), wout_ref[0, HG_WIDTH + ATT_WIDTH:, :]) for r in blocks]
    for r, y in zip(blocks, mixed):
        hr = h_ref[r, :] + _rms(y, g[3:4, :])
        o_ref[r, :] = hr
        x_ref[r, :] = _rms(hr, g[4:5, :]).astype(BF16)
    for c in range(chunks):
        for stage in slots[c]:
            stage()
        gate = _dot(x_ref[...], wgu_ref[0, :, c * FF_CHUNK:(c + 1) * FF_CHUNK])
        up = _dot(x_ref[...], wgu_ref[0, :, D_FF + c * FF_CHUNK:D_FF + (c + 1) * FF_CHUNK])
        act_ref[:, c * FF_CHUNK:(c + 1) * FF_CHUNK] = (gate * jax.nn.sigmoid(gate) * up).astype(BF16)
    finish()
    down = [_dot(act_ref[r, :], wd_ref[0]) for r in blocks]
    gated = []
    for r, y in zip(blocks, down):
        hr = o_ref[r, :] + 0.5 * _rms(y, g[5:6, :])
        o_ref[r, :] = hr
        gate = jax.nn.sigmoid(_dot(_rms(hr, g[6:7, :]).astype(BF16), wgate_ref[0]))
        gated.append(_dot(p_ref[0, r, :].astype(BF16), wproj_ref[0]) * gate)
    for r, eg in zip(blocks, gated):
        o_ref[r, :] = o_ref[r, :] + _rms(eg, g[7:8, :])


def _postmix(h, zh, o_b, o_c, p, lb_logits, norm_gain, gains, w_out, wgu, wd, w_gate, w_proj, layer, seq):
    n = h.shape[0]
    tm = TOKEN_TILE
    nt = n // tm
    w = HG_WIDTH
    ahead = lambda width: pl.BlockSpec((tm, width), lambda i: (jnp.minimum(i, nt - 1), 0))
    behind = lambda width: pl.BlockSpec((tm, width), lambda i: (jnp.maximum(i - 1, 0), 0))
    return pl.pallas_call(
        functools.partial(_postmix_kernel, layer=layer, tiles_per_seq=seq // tm),
        out_shape=jax.ShapeDtypeStruct((n, D_MODEL), F32),
        grid=(nt + 1,),
        in_specs=[ahead(4 * w), _const_spec(lb_logits.shape), _const_spec((1, w)),
                  behind(D_MODEL), behind(ATT_WIDTH), behind(SG_WIDTH),
                  pl.BlockSpec((1, tm, PLE_DIM), lambda i: (layer, jnp.maximum(i - 1, 0), 0)), _layer_spec(gains, layer),
                  _layer_spec(w_out, layer), _layer_spec(wgu, layer), _layer_spec(wd, layer),
                  _layer_spec(w_gate, layer), _layer_spec(w_proj, layer)],
        out_specs=behind(D_MODEL),
        scratch_shapes=[pltpu.VMEM((tm, D_FF), BF16),
                        pltpu.VMEM((tm, D_MODEL), BF16),
                        pltpu.VMEM((tm, w), F32),
                        pltpu.VMEM((tm, w), F32)]
                       + [spec for _, spec in _hgrn_scratch(tm)],
        compiler_params=_params(1, "arbitrary"),
        name="postmix",
    )(zh, lb_logits, norm_gain, h, o_b, o_c, p, gains, w_out, wgu, wd, w_gate, w_proj)


def _rope_table_kernel(pos_ref, inv_ref, sign_ref, cos_ref, sin_ref):
    ang = pos_ref[...] * inv_ref[...]
    cos_ref[...] = jnp.cos(ang)
    sin_ref[...] = jnp.sin(ang) * sign_ref[...]


def _rope_tables(positions):
    n = positions.size
    half = HEAD_DIM // 2
    inv = ROPE_THETA ** (-jnp.arange(half, dtype=F32) / half)
    inv = jnp.tile(inv, LANES // half)[None, :]
    sign = jnp.asarray(np.where((np.arange(LANES) % HEAD_DIM) < half, -1.0, 1.0)[None, :], F32)
    pos = jnp.broadcast_to(positions.astype(F32).reshape(n, 1), (n, LANES))
    tm = min(TOKEN_TILE, n)
    row = pl.BlockSpec((tm, LANES), lambda i: (i, 0))
    return pl.pallas_call(
        _rope_table_kernel,
        out_shape=[jax.ShapeDtypeStruct((n, LANES), F32)] * 2,
        grid=(n // tm,),
        in_specs=[row, _const_spec((1, LANES)), _const_spec((1, LANES))],
        out_specs=[row, row],
        compiler_params=_params(1),
        name="rope_tables",
    )(pos, inv, sign)


def _swap_halves(x):
    w = x.shape[-1]
    half = HEAD_DIM // 2
    lane = lax.broadcasted_iota(jnp.int32, x.shape, 1)
    first = (lane % HEAD_DIM) < half
    return jnp.where(first, pltpu.roll(x, w - half, 1), pltpu.roll(x, half, 1))


def _attention_pieces(sink_ref, q_ref, k_ref, v_ref, o_ref, first_tile):
    blocks = q_ref.shape[0] // WINDOW
    group = ATT_HEADS // ATT_KV_HEADS
    outs = [[None] * ATT_HEADS for _ in range(blocks)]
    order = [(hd, r) for r in range(blocks) for hd in range(ATT_HEADS)]

    def piece(hd, r, last):
        kv = hd // group
        lanes = slice(kv * HEAD_DIM, (kv + 1) * HEAD_DIM)
        keys = slice(r * WINDOW, (r + 2) * WINDOW)
        probs = []

        def scores():
            qi = lax.broadcasted_iota(jnp.int32, (WINDOW, 2 * WINDOW), 0)
            sj = lax.broadcasted_iota(jnp.int32, (WINDOW, 2 * WINDOW), 1)
            rel = qi + WINDOW - sj
            mask = (rel >= 0) & (rel < WINDOW)
            if r == 0:
                mask = mask & ((sj >= WINDOW) | jnp.logical_not(first_tile))
            sink = sink_ref[hd]
            qh = q_ref[r * WINDOW:(r + 1) * WINDOW, hd * HEAD_DIM:(hd + 1) * HEAD_DIM]
            s = _dot_nt(qh, k_ref[keys, lanes]) * (HEAD_DIM ** -0.5)
            s = jnp.where(mask, s, MASK_VALUE)
            m = jnp.maximum(jnp.max(s, axis=-1, keepdims=True), sink)
            pr = jnp.where(mask, jnp.exp(s - m), 0.0)
            denom = jnp.sum(pr, axis=-1, keepdims=True) + jnp.exp(sink - m)
            probs.append((pr * (1.0 / denom)).astype(BF16))

        def values():
            outs[r][hd] = _dot(probs[0], v_ref[keys, lanes])
            if last:
                for rr in range(blocks):
                    o_ref[rr * WINDOW:(rr + 1) * WINDOW, :] = jnp.concatenate(outs[rr], axis=1)

        return scores, values

    return [piece(hd, r, (hd, r) == order[-1]) for hd, r in order]


def _sgu_pieces(u_ref, v_ref, lng_ref, w_ref, bias_ref, o_ref):
    def piece(c):
        def run():
            rows = slice(c * SG_CHUNK, (c + 1) * SG_CHUNK)
            u = jax.nn.gelu(u_ref[rows, :])
            v = jax.nn.gelu(v_ref[rows, :])
            mu = jnp.mean(v, axis=-1, keepdims=True)
            vc = v - mu
            v = (vc * lax.rsqrt(jnp.mean(vc * vc, axis=-1, keepdims=True) + EPS) * lng_ref[...]).astype(BF16)
            ti = lax.broadcasted_iota(jnp.int32, (SG_CHUNK, SG_CHUNK), 0)
            si = lax.broadcasted_iota(jnp.int32, (SG_CHUNK, SG_CHUNK), 1)
            lane_group = lax.broadcasted_iota(jnp.int32, (SG_CHUNK, SG_WIDTH), 1) // HEAD_DIM
            mix = bias_ref[...]
            for g in range(SG_GROUPS):
                wg = jnp.where(si <= ti, w_ref[g], 0.0).astype(BF16)
                mix = mix + jnp.where(lane_group == g, _dot(wg, v), 0.0)
            o_ref[rows, :] = u * mix
        return run

    return [piece(c) for c in range(u_ref.shape[0] // SG_CHUNK)]


def _group_cumsum(x):
    r = lax.broadcasted_iota(jnp.int32, x.shape, 0) % HG_SUB
    s = 1
    while s < HG_SUB:
        x = x + jnp.where(r >= s, pltpu.roll(x, s, 0), 0.0)
        s *= 2
    return x


def _group_last(x):
    n = x.shape[0]
    r = lax.broadcasted_iota(jnp.int32, x.shape, 0) % HG_SUB
    t = jnp.where(r == HG_SUB - 1, x, 0.0)
    s = 1
    while s < HG_SUB:
        t = t + pltpu.roll(t, n - s, 0)
        s *= 2
    return t


def _head_block_mask(shape):
    r = lax.broadcasted_iota(jnp.int32, shape, 0) // HEAD_DIM
    c = lax.broadcasted_iota(jnp.int32, shape, 1) // HEAD_DIM
    return r == c


def _dot_tn(a, b):
    return lax.dot_general(a, b, (((0,), (0,)), ((), ())), preferred_element_type=F32)


HG_BLOCK = 128
HG_STEPS = 8


def _hgrn_stages(z_ref, lbl_ref, gn_ref, o_ref, s, layer, n_slots):
    w = HG_WIDTH
    tile = z_ref.shape[0]
    groups = tile // HG_SUB
    half = HG_SUB // 2
    piece = groups * half
    pair = 2 * HEAD_DIM
    pairs = w // pair

    def prepare(blk):
        def run():
            rows = slice(blk * HG_BLOCK, (blk + 1) * HG_BLOCK)
            q = z_ref[rows, 0:w]
            x = z_ref[rows, w:2 * w]
            v = z_ref[rows, 2 * w:3 * w]
            lg = lbl_ref[...]
            e = jnp.exp(lg - jnp.max(lg, axis=0, keepdims=True))
            probs = e / jnp.sum(e, axis=0, keepdims=True)
            lb = jnp.sum(probs[0:layer + 1], axis=0, keepdims=True) - probs[0:1]
            lb_floor = jnp.maximum(lb, LB_FLOOR)
            t = jnp.exp(-jnp.abs(x))
            r = 1.0 / (1.0 + t)
            tr = t * r
            pos = x >= 0.0
            logf = jnp.log(lb_floor + (1.0 - lb) * jnp.where(pos, r, tr))
            k = (1.0 - lb) * jnp.where(pos, tr, r) + (lb - lb_floor)
            b = _group_cumsum(logf)
            b_last = _group_last(b)
            s["qd"][rows, :] = (q * jnp.exp(b)).astype(BF16)
            s["kd"][rows, :] = (k * jnp.exp(b_last - b)).astype(BF16)
            s["vb"][rows, :] = v.astype(BF16)
            s["k"][rows, :] = k
            s["b2"][rows, :] = b * LOG2E
            per = HG_BLOCK // HG_SUB
            s["decay"][blk * per:(blk + 1) * per, :] = jnp.exp(b_last.reshape(per, HG_SUB, w)[:, 0, :])
            half_rows = slice(blk * HG_BLOCK // 2, (blk + 1) * HG_BLOCK // 2)
            s["o_lo"][half_rows, :] = jnp.zeros((HG_BLOCK // 2, w), F32)
            s["o_hi"][half_rows, :] = jnp.zeros((HG_BLOCK // 2, w), F32)
        return run

    def increments(gk):
        def run():
            blockdiag = _head_block_mask((pair, pair))
            for i in range(HG_STEPS):
                rows = slice((gk * HG_STEPS + i) * HG_SUB, (gk * HG_STEPS + i + 1) * HG_SUB)
                for p in range(pairs):
                    lanes = slice(p * pair, (p + 1) * pair)
                    upd = _dot_tn(s["vb"][rows, lanes], s["kd"][rows, lanes])
                    s["upd"][gk % 2, i, p] = jnp.where(blockdiag, upd, 0.0)
        return run

    def recur(gk):
        def run():
            st = [s["st"][p] for p in range(pairs)]
            for i in range(HG_STEPS):
                m = gk * HG_STEPS + i
                decay = s["decay"][m:m + 1, :]
                for p in range(pairs):
                    s["seen"][m, p] = st[p].astype(BF16)
                    st[p] = st[p] * decay[:, p * pair:(p + 1) * pair] + s["upd"][gk % 2, i, p]
            for p in range(pairs):
                s["st"][p] = st[p]
        return run

    def cross(gk):
        def run():
            for i in range(HG_STEPS):
                m = gk * HG_STEPS + i
                rows = slice(m * HG_SUB, (m + 1) * HG_SUB)
                for p in range(pairs):
                    lanes = slice(p * pair, (p + 1) * pair)
                    s["raw"][rows, lanes] = _dot_nt(s["qd"][rows, lanes], s["seen"][m, p])
        return run

    grp = lambda a: a.reshape(groups, HG_SUB, w)

    def key_products(j):
        def run():
            b3 = grp(s["b2"][...])
            q3 = grp(z_ref[:, 0:w])
            bj = b3[:, j:j + 1, :]
            kj = grp(s["k"][...])[:, j:j + 1, :]
            row = lax.broadcasted_iota(jnp.int32, (groups, half, w), 1)
            if j < half:
                dec = jnp.exp2(jnp.where(row >= j, b3[:, :half] - bj, MASK_VALUE))
                s["t"][j % 2, 0] = (q3[:, :half] * kj * dec).reshape(piece, w).astype(BF16)
                dec = jnp.exp2(b3[:, half:] - bj)
            else:
                dec = jnp.exp2(jnp.where(row >= j - half, b3[:, half:] - bj, MASK_VALUE))
            s["t"][j % 2, 1] = (q3[:, half:] * kj * dec).reshape(piece, w).astype(BF16)
        return run

    def key_sums(j):
        def run():
            vj = grp(z_ref[:, 2 * w:3 * w])[:, j:j + 1, :]
            ones_bd = jnp.where(_head_block_mask((w, w)), 1.0, 0.0).astype(BF16)
            head_sums = lambda t: _dot(t, ones_bd).reshape(groups, half, w)
            if j < half:
                s["o_lo"][...] += (head_sums(s["t"][j % 2, 0]) * vj).reshape(piece, w)
            s["o_hi"][...] += (head_sums(s["t"][j % 2, 1]) * vj).reshape(piece, w)
        return run

    def finish():
        ones_bd = jnp.where(_head_block_mask((w, w)), 1.0, 0.0).astype(BF16)
        intra = jnp.concatenate([s["o_lo"][...].reshape(groups, half, w), s["o_hi"][...].reshape(groups, half, w)],
                                axis=1).reshape(tile, w)
        o = s["raw"][...] + intra
        sq = o * o
        hi = sq.astype(BF16)
        lo = (sq - hi.astype(F32)).astype(BF16)
        ms = (_dot(hi, ones_bd) + _dot(lo, ones_bd)) * (1.0 / HEAD_DIM)
        gate = jax.nn.sigmoid(z_ref[:, 3 * w:4 * w])
        o_ref[...] = o * lax.rsqrt(ms + EPS) * gn_ref[...] * gate

    n_groups = groups // HG_STEPS
    assert tile // HG_BLOCK == n_groups and n_groups < n_slots
    slots = [[] for _ in range(n_slots)]
    for t in range(n_groups + 3):
        slot = slots[min(t, n_slots - 1)]
        if t < n_groups:
            slot.append(prepare(t))
        if 0 <= t - 1 < n_groups:
            slot.append(increments(t - 1))
        if 0 <= t - 2 < n_groups:
            slot.append(recur(t - 2))
        if 0 <= t - 3 < n_groups:
            slot.append(cross(t - 3))
    keys = [key_products(0)]
    for j in range(1, HG_SUB):
        keys += [key_products(j), key_sums(j - 1)]
    keys.append(key_sums(HG_SUB - 1))
    key_slots = n_slots - n_groups
    for n, stage in enumerate(keys):
        slots[n_groups + n * key_slots // len(keys)].append(stage)
    return slots, finish


def _hgrn_scratch(tile):
    w = HG_WIDTH
    pair = 2 * HEAD_DIM
    steps = tile // HG_SUB
    return [("st", pltpu.VMEM((w // pair, pair, pair), F32)),
            ("qd", pltpu.VMEM((tile, w), BF16)),
            ("kd", pltpu.VMEM((tile, w), BF16)),
            ("vb", pltpu.VMEM((tile, w), BF16)),
            ("k", pltpu.VMEM((tile, w), F32)),
            ("b2", pltpu.VMEM((tile, w), F32)),
            ("decay", pltpu.VMEM((steps, w), F32)),
            ("upd", pltpu.VMEM((2, HG_STEPS, w // pair, pair, pair), F32)),
            ("seen", pltpu.VMEM((steps, w // pair, pair, pair), BF16)),
            ("t", pltpu.VMEM((2, 2, tile // 2, w), BF16)),
            ("raw", pltpu.VMEM((tile, w), F32)),
            ("o_lo", pltpu.VMEM((tile // 2, w), F32)),
            ("o_hi", pltpu.VMEM((tile // 2, w), F32))]


def kernel(x, p, positions, norm_gains, w_in, w_out, ffn1_gate_up, ffn1_down, ffn2_gate_up, ffn2_down,
           hgrn_lb_logits, hgrn_norm_gain, attn_sinks, sg_ln_gain, sg_spatial_w, sg_spatial_b, ple_proj, ple_gate):
    batch, seq, _ = x.shape
    depth = norm_gains.shape[0]
    n = batch * seq
    h = x.reshape(n, D_MODEL)
    cos, sin = _rope_tables(positions)

    bf = lambda w: w.astype(BF16)
    wgu1, wd1, win, wgu2, wd2 = bf(ffn1_gate_up), bf(ffn1_down), bf(w_in), bf(ffn2_gate_up), bf(ffn2_down)
    wout, wgate, wproj = bf(w_out), bf(ple_gate), bf(ple_proj)
    for l in range(depth):
        bias_tab = jnp.repeat(sg_spatial_b[l].T, HEAD_DIM, axis=1)
        h, zh, o_b, o_c = _premix(h, norm_gains, wgu1, wd1, win, attn_sinks[l], cos, sin, sg_ln_gain[l][None, :],
                                  sg_spatial_w[l], bias_tab, l, seq)
        h = _postmix(h, zh, o_b, o_c, p.reshape(depth, n, PLE_DIM), hgrn_lb_logits,
                     jnp.tile(hgrn_norm_gain[l], HG_HEADS)[None, :], norm_gains, wout, wgu2, wd2, wgate, wproj, l, seq)
    return h.reshape(batch, seq, D_MODEL)
```

```python
import functools

import numpy as np
import jax
import jax.numpy as jnp
from jax import lax
from jax.experimental import pallas as pl
from jax.experimental.pallas import tpu as pltpu

F32 = jnp.float32
BF16 = jnp.bfloat16

D_MODEL = 1024
HEAD_DIM = 64
HG_HEADS = 4
HG_WIDTH = HG_HEADS * HEAD_DIM
ATT_HEADS = 8
ATT_KV_HEADS = 2
ATT_WIDTH = ATT_HEADS * HEAD_DIM
ATT_KV_WIDTH = ATT_KV_HEADS * HEAD_DIM
WINDOW = 128
ROPE_THETA = 10000.0
SG_GROUPS = 4
SG_WIDTH = SG_GROUPS * HEAD_DIM
SG_CHUNK = 128
IN_WIDTH = 4 * HG_WIDTH + ATT_WIDTH + 2 * ATT_KV_WIDTH + 2 * SG_WIDTH
D_FF = 2816
PLE_DIM = 256
EPS = 1e-6
MASK_VALUE = -1e30
LB_FLOOR = 1e-30
LOG2E = 1.4426950408889634

COL_HG = 0
COL_AQ = 4 * HG_WIDTH
COL_AK = COL_AQ + ATT_WIDTH
COL_AV = COL_AK + ATT_KV_WIDTH
COL_SU = COL_AV + ATT_KV_WIDTH
COL_SV = COL_SU + SG_WIDTH

LANES = 128
TOKEN_TILE = 512
FF_CHUNK = 256
ROW_SPLIT = 2
ATT_TILE = 512
HG_SUB = 16
VMEM_LIMIT = 56 * 1024 * 1024


def _const_spec(shape):
    nd = len(shape)
    return pl.BlockSpec(shape, lambda *_: (0,) * nd, pipeline_mode=pl.Buffered(1))


def _rms(x, g):
    ms = jnp.mean(x * x, axis=-1, keepdims=True)
    return x * lax.rsqrt(ms + EPS) * g


def _dot(a, b):
    return jnp.dot(a, b, preferred_element_type=F32)


def _dot_nt(a, b):
    return lax.dot_general(a, b, (((1,), (1,)), ((), ())), preferred_element_type=F32)


def _params(n_axes, semantics="parallel"):
    return pltpu.CompilerParams(dimension_semantics=(semantics,) * n_axes, vmem_limit_bytes=VMEM_LIMIT)


def _row_blocks(rows):
    return [slice(b * rows // ROW_SPLIT, (b + 1) * rows // ROW_SPLIT) for b in range(ROW_SPLIT)]


def _layer_spec(a, layer):
    return pl.BlockSpec((1,) + a.shape[1:], lambda i: (layer,) + (0,) * (a.ndim - 1), pipeline_mode=pl.Buffered(1))


def _premix_kernel(sink_ref, h_ref, g_ref, wgu_ref, wd_ref, win_ref, cos_ref, sin_ref, lng_ref, ws_ref, bias_ref,
                   o_ref, zh_ref, ob_ref, oc_ref, act_ref, q_ref, k_ref, v_ref, su_ref, sv_ref, *, tiles_per_seq):
    i = pl.program_id(0)

    @pl.when(i == 0)
    def _():
        q_ref[...] = jnp.zeros_like(q_ref)
        k_ref[...] = jnp.zeros_like(k_ref)
        v_ref[...] = jnp.zeros_like(v_ref)
        su_ref[...] = jnp.zeros_like(su_ref)
        sv_ref[...] = jnp.zeros_like(sv_ref)

    first_tile = (i - 1) % tiles_per_seq == 0
    attend = _attention_pieces(sink_ref, q_ref, k_ref, v_ref, ob_ref, first_tile)
    gating = _sgu_pieces(su_ref, sv_ref, lng_ref, ws_ref, bias_ref, oc_ref)

    g = g_ref[0]
    blocks = _row_blocks(h_ref.shape[0])
    x = _rms(h_ref[...], g[0:1, :]).astype(BF16)
    chunks = D_FF // FF_CHUNK
    per_chunk = -(-len(attend) // chunks)
    for c in range(chunks):
        mine = attend[c * per_chunk:(c + 1) * per_chunk]
        for scores, _ in mine:
            scores()
        gate = _dot(x, wgu_ref[0, :, c * FF_CHUNK:(c + 1) * FF_CHUNK])
        up = _dot(x, wgu_ref[0, :, D_FF + c * FF_CHUNK:D_FF + (c + 1) * FF_CHUNK])
        act_ref[:, c * FF_CHUNK:(c + 1) * FF_CHUNK] = (gate * jax.nn.sigmoid(gate) * up).astype(BF16)
        for _, values in mine:
            values()
        if c < len(gating):
            gating[c]()
    k_ref[0:WINDOW, :] = k_ref[ATT_TILE:ATT_TILE + WINDOW, :]
    v_ref[0:WINDOW, :] = v_ref[ATT_TILE:ATT_TILE + WINDOW, :]
    down =[_dot(act_ref[r, :], wd_ref[0]) for r in blocks]
    reps = ATT_WIDTH // LANES
    for r, y in zip(blocks, down):
        hr = h_ref[r, :] + 0.5 * _rms(y, g[1:2, :])
        o_ref[r, :] = hr
        z = _dot(_rms(hr, g[2:3, :]).astype(BF16), win_ref[0])
        zh_ref[r, :] = z[:, COL_HG:COL_AQ]
        su_ref[r, :] = z[:, COL_SU:COL_SV]
        sv_ref[r, :] = z[:, COL_SV:]
        cos, sin = cos_ref[r, :], sin_ref[r, :]
        zq = z[:, COL_AQ:COL_AK]
        zk = z[:, COL_AK:COL_AV]
        kv_rows = slice(WINDOW + r.start, WINDOW + r.stop)
        q_ref[r, :] = (zq * jnp.concatenate([cos] * reps, axis=1)
                       + _swap_halves(zq) * jnp.concatenate([sin] * reps, axis=1)).astype(BF16)
        k_ref[kv_rows, :] = (zk * cos + _swap_halves(zk) * sin).astype(BF16)
        v_ref[kv_rows, :] = z[:, COL_AV:COL_SU].astype(BF16)


def _premix(h, gains, wgu, wd, w_in, sinks, cos, sin, sg_ln, sg_w, sg_bias, layer, seq):
    n = h.shape[0]
    tm = ATT_TILE
    nt = n // tm
    row = lambda w: pl.BlockSpec((tm, w), lambda i: (jnp.minimum(i, nt - 1), 0))
    behind = lambda w: pl.BlockSpec((tm, w), lambda i: (jnp.maximum(i - 1, 0), 0))
    return pl.pallas_call(
        functools.partial(_premix_kernel, tiles_per_seq=seq // tm),
        out_shape=[jax.ShapeDtypeStruct((n, D_MODEL), F32), jax.ShapeDtypeStruct((n, COL_AQ), F32),
                   jax.ShapeDtypeStruct((n, ATT_WIDTH), F32), jax.ShapeDtypeStruct((n, SG_WIDTH), F32)],
        grid=(nt + 1,),
        in_specs=[pl.BlockSpec(memory_space=pltpu.SMEM), row(D_MODEL), _layer_spec(gains, layer),
                  _layer_spec(wgu, layer), _layer_spec(wd, layer), _layer_spec(w_in, layer), row(LANES), row(LANES),
                  _const_spec(sg_ln.shape), _const_spec(sg_w.shape), _const_spec(sg_bias.shape)],
        out_specs=[row(D_MODEL), row(COL_AQ), behind(ATT_WIDTH), behind(SG_WIDTH)],
        scratch_shapes=[pltpu.VMEM((tm, D_FF), BF16),
                        pltpu.VMEM((tm, ATT_WIDTH), BF16),
                        pltpu.VMEM((tm + WINDOW, ATT_KV_WIDTH), BF16),
                        pltpu.VMEM((tm + WINDOW, ATT_KV_WIDTH), BF16),
                        pltpu.VMEM((tm, SG_WIDTH), F32),
                        pltpu.VMEM((tm, SG_WIDTH), F32)],
        compiler_params=_params(1, "arbitrary"),
        name="premix",
    )(sinks, h, gains, wgu, wd, w_in, cos, sin, sg_ln, sg_w, sg_bias)


def _postmix_kernel(zh_ref, lbl_ref, gn_ref, h_ref, ob_ref, oc_ref, p_ref, g_ref, wout_ref, wgu_ref, wd_ref,
                    wgate_ref, wproj_ref, o_ref, act_ref, x_ref, oa_ref, oa_next_ref, *hgrn_refs, layer, tiles_per_seq):
    i = pl.program_id(0)
    s = dict(zip([name for name, _ in _hgrn_scratch(zh_ref.shape[0])], hgrn_refs))

    @pl.when(i % tiles_per_seq == 0)
    def _():
        s["st"][...] = jnp.zeros_like(s["st"])

    @pl.when(i == 0)
    def _():
        oa_next_ref[...] = jnp.zeros_like(oa_next_ref)

    oa_ref[...] = oa_next_ref[...]
    chunks = D_FF // FF_CHUNK
    slots, finish = _hgrn_stages(zh_ref, lbl_ref, gn_ref, oa_next_ref, s, layer, chunks)

    g = g_ref[0]
    blocks = _row_blocks(h_ref.shape[0])
    mixed = [_dot(oa_ref[r, :].astype(BF16), wout_ref[0, 0:HG_WIDTH, :])
             + _dot(ob_ref[r, :].astype(BF16), wout_ref[0, HG_WIDTH:HG_WIDTH + ATT_WIDTH, :])
             + _dot(oc_ref[r, :].astype(BF16), wout_ref[0, HG_WIDTH + ATT_WIDTH:, :]) for r in blocks]
    for r, y in zip(blocks, mixed):
        hr = h_ref[r, :] + _rms(y, g[3:4, :])
        o_ref[r, :] = hr
        x_ref[r, :] = _rms(hr, g[4:5, :]).astype(BF16)
    for c in range(chunks):
        for stage in slots[c]:
            stage()
        for r in blocks:
            gate = _dot(x_ref[r, :], wgu_ref[0, :, c * FF_CHUNK:(c + 1) * FF_CHUNK])
            up = _dot(x_ref[r, :], wgu_ref[0, :, D_FF + c * FF_CHUNK:D_FF + (c + 1) * FF_CHUNK])
            act_ref[r, c * FF_CHUNK:(c + 1) * FF_CHUNK] = (gate * jax.nn.sigmoid(gate) * up).astype(BF16)
    finish()
    down = [_dot(act_ref[r, :], wd_ref[0]) for r in blocks]
    gated = []
    for r, y in zip(blocks, down):
        hr = o_ref[r, :] + 0.5 * _rms(y, g[5:6, :])
        o_ref[r, :] = hr
        gate = jax.nn.sigmoid(_dot(_rms(hr, g[6:7, :]).astype(BF16), wgate_ref[0]))
        gated.append(_dot(p_ref[0, r, :].astype(BF16), wproj_ref[0]) * gate)
    for r, eg in zip(blocks, gated):
        o_ref[r, :] = o_ref[r, :] + _rms(eg, g[7:8, :])


def _postmix(h, zh, o_b, o_c, p, lb_logits, norm_gain, gains, w_out, wgu, wd, w_gate, w_proj, layer, seq):
    n = h.shape[0]
    tm = TOKEN_TILE
    nt = n // tm
    w = HG_WIDTH
    ahead = lambda width: pl.BlockSpec((tm, width), lambda i: (jnp.minimum(i, nt - 1), 0))
    behind = lambda width: pl.BlockSpec((tm, width), lambda i: (jnp.maximum(i - 1, 0), 0))
    return pl.pallas_call(
        functools.partial(_postmix_kernel, layer=layer, tiles_per_seq=seq // tm),
        out_shape=jax.ShapeDtypeStruct((n, D_MODEL), F32),
        grid=(nt + 1,),
        in_specs=[ahead(4 * w), _const_spec(lb_logits.shape), _const_spec((1, w)),
                  behind(D_MODEL), behind(ATT_WIDTH), behind(SG_WIDTH),
                  pl.BlockSpec((1, tm, PLE_DIM), lambda i: (layer, jnp.maximum(i - 1, 0), 0)), _layer_spec(gains, layer),
                  _layer_spec(w_out, layer), _layer_spec(wgu, layer), _layer_spec(wd, layer),
                  _layer_spec(w_gate, layer), _layer_spec(w_proj, layer)],
        out_specs=behind(D_MODEL),
        scratch_shapes=[pltpu.VMEM((tm, D_FF), BF16),
                        pltpu.VMEM((tm, D_MODEL), BF16),
                        pltpu.VMEM((tm, w), F32),
                        pltpu.VMEM((tm, w), F32)]
                       + [spec for _, spec in _hgrn_scratch(tm)],
        compiler_params=_params(1, "arbitrary"),
        name="postmix",
    )(zh, lb_logits, norm_gain, h, o_b, o_c, p, gains, w_out, wgu, wd, w_gate, w_proj)


def _rope_table_kernel(pos_ref, inv_ref, sign_ref, cos_ref, sin_ref):
    ang = pos_ref[...] * inv_ref[...]
    cos_ref[...] = jnp.cos(ang)
    sin_ref[...] = jnp.sin(ang) * sign_ref[...]


def _rope_tables(positions):
    n = positions.size
    half = HEAD_DIM // 2
    inv = ROPE_THETA ** (-jnp.arange(half, dtype=F32) / half)
    inv = jnp.tile(inv, LANES // half)[None, :]
    sign = jnp.asarray(np.where((np.arange(LANES) % HEAD_DIM) < half, -1.0, 1.0)[None, :], F32)
    pos = jnp.broadcast_to(positions.astype(F32).reshape(n, 1), (n, LANES))
    tm = min(TOKEN_TILE, n)
    row = pl.BlockSpec((tm, LANES), lambda i: (i, 0))
    return pl.pallas_call(
        _rope_table_kernel,
        out_shape=[jax.ShapeDtypeStruct((n, LANES), F32)] * 2,
        grid=(n // tm,),
        in_specs=[row, _const_spec((1, LANES)), _const_spec((1, LANES))],
        out_specs=[row, row],
        compiler_params=_params(1),
        name="rope_tables",
    )(pos, inv, sign)


def _swap_halves(x):
    w = x.shape[-1]
    half = HEAD_DIM // 2
    lane = lax.broadcasted_iota(jnp.int32, x.shape, 1)
    first = (lane % HEAD_DIM) < half
    return jnp.where(first, pltpu.roll(x, w - half, 1), pltpu.roll(x, half, 1))


def _attention_pieces(sink_ref, q_ref, k_ref, v_ref, o_ref, first_tile):
    blocks = q_ref.shape[0] // WINDOW
    group = ATT_HEADS // ATT_KV_HEADS
    outs = [[None] * ATT_HEADS for _ in range(blocks)]
    order = [(hd, r) for r in range(blocks) for hd in range(ATT_HEADS)]

    def piece(hd, r, last):
        kv = hd // group
        lanes = slice(kv * HEAD_DIM, (kv + 1) * HEAD_DIM)
        keys = slice(r * WINDOW, (r + 2) * WINDOW)
        probs = []

        def scores():
            qi = lax.broadcasted_iota(jnp.int32, (WINDOW, 2 * WINDOW), 0)
            sj = lax.broadcasted_iota(jnp.int32, (WINDOW, 2 * WINDOW), 1)
            rel = qi + WINDOW - sj
            mask = (rel >= 0) & (rel < WINDOW)
            if r == 0:
                mask = mask & ((sj >= WINDOW) | jnp.logical_not(first_tile))
            sink = sink_ref[hd]
            qh = q_ref[r * WINDOW:(r + 1) * WINDOW, hd * HEAD_DIM:(hd + 1) * HEAD_DIM]
            s = _dot_nt(qh, k_ref[keys, lanes]) * (HEAD_DIM ** -0.5)
            s = jnp.where(mask, s, MASK_VALUE)
            m = jnp.maximum(jnp.max(s, axis=-1, keepdims=True), sink)
            pr = jnp.where(mask, jnp.exp(s - m), 0.0)
            denom = jnp.sum(pr, axis=-1, keepdims=True) + jnp.exp(sink - m)
            probs.append((pr * (1.0 / denom)).astype(BF16))

        def values():
            outs[r][hd] = _dot(probs[0], v_ref[keys, lanes])
            if last:
                for rr in range(blocks):
                    o_ref[rr * WINDOW:(rr + 1) * WINDOW, :] = jnp.concatenate(outs[rr], axis=1)

        return scores, values

    return [piece(hd, r, (hd, r) == order[-1]) for hd, r in order]


def _sgu_pieces(u_ref, v_ref, lng_ref, w_ref, bias_ref, o_ref):
    def piece(c):
        def run():
            rows = slice(c * SG_CHUNK, (c + 1) * SG_CHUNK)
            u = jax.nn.gelu(u_ref[rows, :])
            v = jax.nn.gelu(v_ref[rows, :])
            mu = jnp.mean(v, axis=-1, keepdims=True)
            vc = v - mu
            v = (vc * lax.rsqrt(jnp.mean(vc * vc, axis=-1, keepdims=True) + EPS) * lng_ref[...]).astype(BF16)
            ti = lax.broadcasted_iota(jnp.int32, (SG_CHUNK, SG_CHUNK), 0)
            si = lax.broadcasted_iota(jnp.int32, (SG_CHUNK, SG_CHUNK), 1)
            lane_group = lax.broadcasted_iota(jnp.int32, (SG_CHUNK, SG_WIDTH), 1) // HEAD_DIM
            mix = bias_ref[...]
            for g in range(SG_GROUPS):
                wg = jnp.where(si <= ti, w_ref[g], 0.0).astype(BF16)
                mix = mix + jnp.where(lane_group == g, _dot(wg, v), 0.0)
            o_ref[rows, :] = u * mix
        return run

    return [piece(c) for c in range(u_ref.shape[0] // SG_CHUNK)]


def _group_cumsum(x):
    r = lax.broadcasted_iota(jnp.int32, x.shape, 0) % HG_SUB
    s = 1
    while s < HG_SUB:
        x = x + jnp.where(r >= s, pltpu.roll(x, s, 0), 0.0)
        s *= 2
    return x


def _group_last(x):
    n = x.shape[0]
    r = lax.broadcasted_iota(jnp.int32, x.shape, 0) % HG_SUB
    t = jnp.where(r == HG_SUB - 1, x, 0.0)
    s = 1
    while s < HG_SUB:
        t = t + pltpu.roll(t, n - s, 0)
        s *= 2
    return t


def _head_block_mask(shape):
    r = lax.broadcasted_iota(jnp.int32, shape, 0) // HEAD_DIM
    c = lax.broadcasted_iota(jnp.int32, shape, 1) // HEAD_DIM
    return r == c


def _dot_tn(a, b):
    return lax.dot_general(a, b, (((0,), (0,)), ((), ())), preferred_element_type=F32)


HG_BLOCK = 128
HG_STEPS = 8


def _hgrn_stages(z_ref, lbl_ref, gn_ref, o_ref, s, layer, n_slots):
    w = HG_WIDTH
    tile = z_ref.shape[0]
    groups = tile // HG_SUB
    half = HG_SUB // 2
    piece = groups * half
    pair = 2 * HEAD_DIM
    pairs = w // pair

    def prepare(blk):
        def run():
            rows = slice(blk * HG_BLOCK, (blk + 1) * HG_BLOCK)
            q = z_ref[rows, 0:w]
            x = z_ref[rows, w:2 * w]
            v = z_ref[rows, 2 * w:3 * w]
            lg = lbl_ref[...]
            e = jnp.exp(lg - jnp.max(lg, axis=0, keepdims=True))
            probs = e / jnp.sum(e, axis=0, keepdims=True)
            lb = jnp.sum(probs[0:layer + 1], axis=0, keepdims=True) - probs[0:1]
            lb_floor = jnp.maximum(lb, LB_FLOOR)
            t = jnp.exp(-jnp.abs(x))
            r = 1.0 / (1.0 + t)
            tr = t * r
            pos = x >= 0.0
            logf = jnp.log(lb_floor + (1.0 - lb) * jnp.where(pos, r, tr))
            k = (1.0 - lb) * jnp.where(pos, tr, r) + (lb - lb_floor)
            b = _group_cumsum(logf)
            b_last = _group_last(b)
            s["qd"][rows, :] = (q * jnp.exp(b)).astype(BF16)
            s["kd"][rows, :] = (k * jnp.exp(b_last - b)).astype(BF16)
            s["vb"][rows, :] = v.astype(BF16)
            s["k"][rows, :] = k
            s["b2"][rows, :] = b * LOG2E
            per = HG_BLOCK // HG_SUB
            s["decay"][blk * per:(blk + 1) * per, :] = jnp.exp(b_last.reshape(per, HG_SUB, w)[:, 0, :])
            half_rows = slice(blk * HG_BLOCK // 2, (blk + 1) * HG_BLOCK // 2)
            s["o_lo"][half_rows, :] = jnp.zeros((HG_BLOCK // 2, w), F32)
            s["o_hi"][half_rows, :] = jnp.zeros((HG_BLOCK // 2, w), F32)
        return run

    def increments(gk):
        def run():
            blockdiag = _head_block_mask((pair, pair))
            for i in range(HG_STEPS):
                rows = slice((gk * HG_STEPS + i) * HG_SUB, (gk * HG_STEPS + i + 1) * HG_SUB)
                for p in range(pairs):
                    lanes = slice(p * pair, (p + 1) * pair)
                    upd = _dot_tn(s["vb"][rows, lanes], s["kd"][rows, lanes])
                    s["upd"][gk % 2, i, p] = jnp.where(blockdiag, upd, 0.0)
        return run

    def recur(gk):
        def run():
            st = [s["st"][p] for p in range(pairs)]
            for i in range(HG_STEPS):
                m = gk * HG_STEPS + i
                decay = s["decay"][m:m + 1, :]
                for p in range(pairs):
                    s["seen"][m, p] = st[p].astype(BF16)
                    st[p] = st[p] * decay[:, p * pair:(p + 1) * pair] + s["upd"][gk % 2, i, p]
            for p in range(pairs):
                s["st"][p] = st[p]
        return run

    def cross(gk):
        def run():
            for i in range(HG_STEPS):
                m = gk * HG_STEPS + i
                rows = slice(m * HG_SUB, (m + 1) * HG_SUB)
                for p in range(pairs):
                    lanes = slice(p * pair, (p + 1) * pair)
                    s["raw"][rows, lanes] = _dot_nt(s["qd"][rows, lanes], s["seen"][m, p])
        return run

    grp = lambda a: a.reshape(groups, HG_SUB, w)

    def key_products(j):
        def run():
            b3 = grp(s["b2"][...])
            q3 = grp(z_ref[:, 0:w])
            bj = b3[:, j:j + 1, :]
            kj = grp(s["k"][...])[:, j:j + 1, :]
            row = lax.broadcasted_iota(jnp.int32, (groups, half, w), 1)
            if j < half:
                dec = jnp.exp2(jnp.where(row >= j, b3[:, :half] - bj, MASK_VALUE))
                s["t"][j % 2, 0] = (q3[:, :half] * kj * dec).reshape(piece, w).astype(BF16)
                dec = jnp.exp2(b3[:, half:] - bj)
            else:
                dec = jnp.exp2(jnp.where(row >= j - half, b3[:, half:] - bj, MASK_VALUE))
            s["t"][j % 2, 1] = (q3[:, half:] * kj * dec).reshape(piece, w).astype(BF16)
        return run

    def key_sums(j):
        def run():
            vj = grp(z_ref[:, 2 * w:3 * w])[:, j:j + 1, :]
            ones_bd = jnp.where(_head_block_mask((w, w)), 1.0, 0.0).astype(BF16)
            head_sums = lambda t: _dot(t, ones_bd).reshape(groups, half, w)
            if j < half:
                s["o_lo"][...] += (head_sums(s["t"][j % 2, 0]) * vj).reshape(piece, w)
            s["o_hi"][...] += (head_sums(s["t"][j % 2, 1]) * vj).reshape(piece, w)
        return run

    def finish():
        ones_bd = jnp.where(_head_block_mask((w, w)), 1.0, 0.0).astype(BF16)
        intra = jnp.concatenate([s["o_lo"][...].reshape(groups, half, w), s["o_hi"][...].reshape(groups, half, w)],
                                axis=1).reshape(tile, w)
        o = s["raw"][...] + intra
        sq = o * o
        hi = sq.astype(BF16)
        lo = (sq - hi.astype(F32)).astype(BF16)
        ms = (_dot(hi, ones_bd) + _dot(lo, ones_bd)) * (1.0 / HEAD_DIM)
        gate = jax.nn.sigmoid(z_ref[:, 3 * w:4 * w])
        o_ref[...] = o * lax.rsqrt(ms + EPS) * gn_ref[...] * gate

    n_groups = groups // HG_STEPS
    assert tile // HG_BLOCK == n_groups and n_groups < n_slots
    slots = [[] for _ in range(n_slots)]
    for t in range(n_groups + 3):
        slot = slots[min(t, n_slots - 1)]
        if t < n_groups:
            slot.append(prepare(t))
        if 0 <= t - 1 < n_groups:
            slot.append(increments(t - 1))
        if 0 <= t - 2 < n_groups:
            slot.append(recur(t - 2))
        if 0 <= t - 3 < n_groups:
            slot.append(cross(t - 3))
    keys = [key_products(0)]
    for j in range(1, HG_SUB):
        keys += [key_products(j), key_sums(j - 1)]
    keys.append(key_sums(HG_SUB - 1))
    key_slots = n_slots - n_groups
    for n, stage in enumerate(keys):
        slots[n_groups + n * key_slots // len(keys)].append(stage)
    return slots, finish


def _hgrn_scratch(tile):
    w = HG_WIDTH
    pair = 2 * HEAD_DIM
    steps = tile // HG_SUB
    return [("st", pltpu.VMEM((w // pair, pair, pair), F32)),
            ("qd", pltpu.VMEM((tile, w), BF16)),
            ("kd", pltpu.VMEM((tile, w), BF16)),
            ("vb", pltpu.VMEM((tile, w), BF16)),
            ("k", pltpu.VMEM((tile, w), F32)),
            ("b2", pltpu.VMEM((tile, w), F32)),
            ("decay", pltpu.VMEM((steps, w), F32)),
            ("upd", pltpu.VMEM((2, HG_STEPS, w // pair, pair, pair), F32)),
            ("seen", pltpu.VMEM((steps, w // pair, pair, pair), BF16)),
            ("t", pltpu.VMEM((2, 2, tile // 2, w), BF16)),
            ("raw", pltpu.VMEM((tile, w), F32)),
            ("o_lo", pltpu.VMEM((tile // 2, w), F32)),
            ("o_hi", pltpu.VMEM((tile // 2, w), F32))]


def kernel(x, p, positions, norm_gains, w_in, w_out, ffn1_gate_up, ffn1_down, ffn2_gate_up, ffn2_down,
           hgrn_lb_logits, hgrn_norm_gain, attn_sinks, sg_ln_gain, sg_spatial_w, sg_spatial_b, ple_proj, ple_gate):
    batch, seq, _ = x.shape
    depth = norm_gains.shape[0]
    n = batch * seq
    h = x.reshape(n, D_MODEL)
    cos, sin = _rope_tables(positions)

    bf = lambda w: w.astype(BF16)
    wgu1, wd1, win, wgu2, wd2 = bf(ffn1_gate_up), bf(ffn1_down), bf(w_in), bf(ffn2_gate_up), bf(ffn2_down)
    wout, wgate, wproj = bf(w_out), bf(ple_gate), bf(ple_proj)
    for l in range(depth):
        bias_tab = jnp.repeat(sg_spatial_b[l].T, HEAD_DIM, axis=1)
        h, zh, o_b, o_c = _premix(h, norm_gains, wgu1, wd1, win, attn_sinks[l], cos, sin, sg_ln_gain[l][None, :],
                                  sg_spatial_w[l], bias_tab, l, seq)
        h = _postmix(h, zh, o_b, o_c, p.reshape(depth, n, PLE_DIM), hgrn_lb_logits,
                     jnp.tile(hgrn_norm_gain[l], HG_HEADS)[None, :], norm_gains, wout, wgu2, wd2, wgate, wproj, l, seq)
    return h.reshape(batch, seq, D_MODEL)
```

```python
import functools

import numpy as np
import jax
import jax.numpy as jnp
from jax import lax
from jax.experimental import pallas as pl
from jax.experimental.pallas import tpu as pltpu

F32 = jnp.float32
BF16 = jnp.bfloat16

D_MODEL = 1024
HEAD_DIM = 64
HG_HEADS = 4
HG_WIDTH = HG_HEADS * HEAD_DIM
ATT_HEADS = 8
ATT_KV_HEADS = 2
ATT_WIDTH = ATT_HEADS * HEAD_DIM
ATT_KV_WIDTH = ATT_KV_HEADS * HEAD_DIM
WINDOW = 128
ROPE_THETA = 10000.0
SG_GROUPS = 4
SG_WIDTH = SG_GROUPS * HEAD_DIM
SG_CHUNK = 128
IN_WIDTH = 4 * HG_WIDTH + ATT_WIDTH + 2 * ATT_KV_WIDTH + 2 * SG_WIDTH
D_FF = 2816
PLE_DIM = 256
EPS = 1e-6
MASK_VALUE = -1e30
LB_FLOOR = 1e-30
LOG2E = 1.4426950408889634

COL_HG = 0
COL_AQ = 4 * HG_WIDTH
COL_AK = COL_AQ + ATT_WIDTH
COL_AV = COL_AK + ATT_KV_WIDTH
COL_SU = COL_AV + ATT_KV_WIDTH
COL_SV = COL_SU + SG_WIDTH

LANES = 128
TOKEN_TILE = 512
FF_CHUNK = 256
ROW_SPLIT = 2
ATT_TILE = 512
HG_SUB = 16
VMEM_LIMIT = 56 * 1024 * 1024


def _const_spec(shape):
    nd = len(shape)
    return pl.BlockSpec(shape, lambda *_: (0,) * nd, pipeline_mode=pl.Buffered(1))


def _rms(x, g):
    ms = jnp.mean(x * x, axis=-1, keepdims=True)
    return x * lax.rsqrt(ms + EPS) * g


def _dot(a, b):
    return jnp.dot(a, b, preferred_element_type=F32)


def _dot_nt(a, b):
    return lax.dot_general(a, b, (((1,), (1,)), ((), ())), preferred_element_type=F32)


def _params(n_axes, semantics="parallel"):
    return pltpu.CompilerParams(dimension_semantics=(semantics,) * n_axes, vmem_limit_bytes=VMEM_LIMIT)


def _row_blocks(rows):
    return [slice(b * rows // ROW_SPLIT, (b + 1) * rows // ROW_SPLIT) for b in range(ROW_SPLIT)]


def _layer_spec(a, layer):
    return pl.BlockSpec((1,) + a.shape[1:], lambda i: (layer,) + (0,) * (a.ndim - 1), pipeline_mode=pl.Buffered(1))


def _premix_kernel(sink_ref, h_ref, g_ref, wgu_ref, wd_ref, win_ref, cos_ref, sin_ref, lng_ref, ws_ref, bias_ref,
                   o_ref, zh_ref, ob_ref, oc_ref, act_ref, q_ref, k_ref, v_ref, su_ref, sv_ref, *, tiles_per_seq):
    i = pl.program_id(0)

    @pl.when(i == 0)
    def _():
        q_ref[...] = jnp.zeros_like(q_ref)
        k_ref[...] = jnp.zeros_like(k_ref)
        v_ref[...] = jnp.zeros_like(v_ref)
        su_ref[...] = jnp.zeros_like(su_ref)
        sv_ref[...] = jnp.zeros_like(sv_ref)

    first_tile = (i - 1) % tiles_per_seq == 0
    attend = _attention_pieces(sink_ref, q_ref, k_ref, v_ref, ob_ref, first_tile)
    gating = _sgu_pieces(su_ref, sv_ref, lng_ref, ws_ref, bias_ref, oc_ref)

    g = g_ref[0]
    blocks = _row_blocks(h_ref.shape[0])
    x = _rms(h_ref[...], g[0:1, :]).astype(BF16)
    chunks = D_FF // FF_CHUNK
    per_chunk = -(-len(attend) // chunks)
    for c in range(chunks):
        mine = attend[c * per_chunk:(c + 1) * per_chunk]
        for scores, _ in mine:
            scores()
        gate = _dot(x, wgu_ref[0, :, c * FF_CHUNK:(c + 1) * FF_CHUNK])
        up = _dot(x, wgu_ref[0, :, D_FF + c * FF_CHUNK:D_FF + (c + 1) * FF_CHUNK])
        act_ref[:, c * FF_CHUNK:(c + 1) * FF_CHUNK] = (gate * jax.nn.sigmoid(gate) * up).astype(BF16)
        for _, values in mine:
            values()
        if c < len(gating):
            gating[c]()
    k_ref[0:WINDOW, :] = k_ref[ATT_TILE:ATT_TILE + WINDOW, :]
    v_ref[0:WINDOW, :] = v_ref[ATT_TILE:ATT_TILE + WINDOW, :]
    down =[_dot(act_ref[r, :], wd_ref[0]) for r in blocks]
    reps = ATT_WIDTH // LANES
    for r, y in zip(blocks, down):
        hr = h_ref[r, :] + 0.5 * _rms(y, g[1:2, :])
        o_ref[r, :] = hr
        z = _dot(_rms(hr, g[2:3, :]).astype(BF16), win_ref[0])
        zh_ref[r, :] = z[:, COL_HG:COL_AQ]
        su_ref[r, :] = z[:, COL_SU:COL_SV]
        sv_ref[r, :] = z[:, COL_SV:]
        cos, sin = cos_ref[r, :], sin_ref[r, :]
        zq = z[:, COL_AQ:COL_AK]
        zk = z[:, COL_AK:COL_AV]
        kv_rows = slice(WINDOW + r.start, WINDOW + r.stop)
        q_ref[r, :] = (zq * jnp.concatenate([cos] * reps, axis=1)
                       + _swap_halves(zq) * jnp.concatenate([sin] * reps, axis=1)).astype(BF16)
        k_ref[kv_rows, :] = (zk * cos + _swap_halves(zk) * sin).astype(BF16)
        v_ref[kv_rows, :] = z[:, COL_AV:COL_SU].astype(BF16)


def _premix(h, gains, wgu, wd, w_in, sinks, cos, sin, sg_ln, sg_w, sg_bias, layer, seq):
    n = h.shape[0]
    tm = ATT_TILE
    nt = n // tm
    row = lambda w: pl.BlockSpec((tm, w), lambda i: (jnp.minimum(i, nt - 1), 0))
    behind = lambda w: pl.BlockSpec((tm, w), lambda i: (jnp.maximum(i - 1, 0), 0))
    return pl.pallas_call(
        functools.partial(_premix_kernel, tiles_per_seq=seq // tm),
        out_shape=[jax.ShapeDtypeStruct((n, D_MODEL), F32), jax.ShapeDtypeStruct((n, COL_AQ), F32),
                   jax.ShapeDtypeStruct((n, ATT_WIDTH), F32), jax.ShapeDtypeStruct((n, SG_WIDTH), F32)],
        grid=(nt + 1,),
        in_specs=[pl.BlockSpec(memory_space=pltpu.SMEM), row(D_MODEL), _layer_spec(gains, layer),
                  _layer_spec(wgu, layer), _layer_spec(wd, layer), _layer_spec(w_in, layer), row(LANES), row(LANES),
                  _const_spec(sg_ln.shape), _const_spec(sg_w.shape), _const_spec(sg_bias.shape)],
        out_specs=[row(D_MODEL), row(COL_AQ), behind(ATT_WIDTH), behind(SG_WIDTH)],
        scratch_shapes=[pltpu.VMEM((tm, D_FF), BF16),
                        pltpu.VMEM((tm, ATT_WIDTH), BF16),
                        pltpu.VMEM((tm + WINDOW, ATT_KV_WIDTH), BF16),
                        pltpu.VMEM((tm + WINDOW, ATT_KV_WIDTH), BF16),
                        pltpu.VMEM((tm, SG_WIDTH), F32),
                        pltpu.VMEM((tm, SG_WIDTH), F32)],
        compiler_params=_params(1, "arbitrary"),
        name="premix",
    )(sinks, h, gains, wgu, wd, w_in, cos, sin, sg_ln, sg_w, sg_bias)


def _postmix_kernel(zh_ref, lbl_ref, gn_ref, h_ref, ob_ref, oc_ref, p_ref, g_ref, wout_ref, wgu_ref, wd_ref,
                    wgate_ref, wproj_ref, o_ref, act_ref, x_ref, oa_ref, oa_next_ref, *hgrn_refs, layer, tiles_per_seq):
    i = pl.program_id(0)
    s = dict(zip([name for name, _ in _hgrn_scratch(zh_ref.shape[0])], hgrn_refs))

    @pl.when(i % tiles_per_seq == 0)
    def _():
        s["st"][...] = jnp.zeros_like(s["st"])

    @pl.when(i == 0)
    def _():
        oa_next_ref[...] = jnp.zeros_like(oa_next_ref)

    oa_ref[...] = oa_next_ref[...]
    chunks = D_FF // FF_CHUNK
    slots, finish = _hgrn_stages(zh_ref, lbl_ref, gn_ref, oa_next_ref, s, layer, chunks)

    g = g_ref[0]
    blocks = _row_blocks(h_ref.shape[0])
    mixed = [_dot(oa_ref[r, :].astype(BF16), wout_ref[0, 0:HG_WIDTH, :])
             + _dot(ob_ref[r, :].astype(BF16), wout_ref[0, HG_WIDTH:HG_WIDTH + ATT_WIDTH, :])
             + _dot(oc_ref[r, :].astype(BF16), wout_ref[0, HG_WIDTH + ATT_WIDTH:, :]) for r in blocks]
    for r, y in zip(blocks, mixed):
        hr = h_ref[r, :] + _rms(y, g[3:4, :])
        o_ref[r, :] = hr
        x_ref[r, :] = _rms(hr, g[4:5, :]).astype(BF16)
    for c in range(chunks):
        for stage in slots[c]:
            stage()
        for r in blocks:
            gate = _dot(x_ref[r, :], wgu_ref[0, :, c * FF_CHUNK:(c + 1) * FF_CHUNK])
            up = _dot(x_ref[r, :], wgu_ref[0, :, D_FF + c * FF_CHUNK:D_FF + (c + 1) * FF_CHUNK])
            act_ref[r, c * FF_CHUNK:(c + 1) * FF_CHUNK] = (gate * jax.nn.sigmoid(gate) * up).astype(BF16)
    finish()
    down = [_dot(act_ref[r, :], wd_ref[0]) for r in blocks]
    gated = []
    for r, y in zip(blocks, down):
        hr = o_ref[r, :] + 0.5 * _rms(y, g[5:6, :])
        o_ref[r, :] = hr
        gate = jax.nn.sigmoid(_dot(_rms(hr, g[6:7, :]).astype(BF16), wgate_ref[0]))
        gated.append(_dot(p_ref[0, r, :].astype(BF16), wproj_ref[0]) * gate)
    for r, eg in zip(blocks, gated):
        o_ref[r, :] = o_ref[r, :] + _rms(eg, g[7:8, :])


def _postmix(h, zh, o_b, o_c, p, lb_logits, norm_gain, gains, w_out, wgu, wd, w_gate, w_proj, layer, seq):
    n = h.shape[0]
    tm = TOKEN_TILE
    nt = n // tm
    w = HG_WIDTH
    ahead = lambda width: pl.BlockSpec((tm, width), lambda i: (jnp.minimum(i, nt - 1), 0))
    behind = lambda width: pl.BlockSpec((tm, width), lambda i: (jnp.maximum(i - 1, 0), 0))
    return pl.pallas_call(
        functools.partial(_postmix_kernel, layer=layer, tiles_per_seq=seq // tm),
        out_shape=jax.ShapeDtypeStruct((n, D_MODEL), F32),
        grid=(nt + 1,),
        in_specs=[ahead(4 * w), _const_spec(lb_logits.shape), _const_spec((1, w)),
                  behind(D_MODEL), behind(ATT_WIDTH), behind(SG_WIDTH),
                  pl.BlockSpec((1, tm, PLE_DIM), lambda i: (layer, jnp.maximum(i - 1, 0), 0)), _layer_spec(gains, layer),
                  _layer_spec(w_out, layer), _layer_spec(wgu, layer), _layer_spec(wd, layer),
                  _layer_spec(w_gate, layer), _layer_spec(w_proj, layer)],
        out_specs=behind(D_MODEL),
        scratch_shapes=[pltpu.VMEM((tm, D_FF), BF16),
                        pltpu.VMEM((tm, D_MODEL), BF16),
                        pltpu.VMEM((tm, w), F32),
                        pltpu.VMEM((tm, w), F32)]
                       + [spec for _, spec in _hgrn_scratch(tm)],
        compiler_params=_params(1, "arbitrary"),
        name="postmix",
    )(zh, lb_logits, norm_gain, h, o_b, o_c, p, gains, w_out, wgu, wd, w_gate, w_proj)


def _rope_table_kernel(pos_ref, inv_ref, sel_ref, sign_ref, cos_ref, sin_ref):
    rows = pos_ref.shape[0]
    per_row = LANES // (HEAD_DIM // 2)
    ang = pos_ref[...] * inv_ref[...]
    for table, out_ref, scale in ((jnp.cos(ang), cos_ref, None), (jnp.sin(ang), sin_ref, sign_ref[...])):
        hi = table.astype(BF16)
        rest = table - hi.astype(F32)
        mid = rest.astype(BF16)
        lo = (rest - mid.astype(F32)).astype(BF16)
        for j in range(per_row):
            spread = _dot(hi, sel_ref[j]) + _dot(mid, sel_ref[j]) + _dot(lo, sel_ref[j])
            out_ref[pl.ds(j, rows, stride=per_row), :] = spread if scale is None else spread * scale


def _rope_tables(positions):
    n = positions.size
    half = HEAD_DIM // 2
    per_row = LANES // half
    inv = ROPE_THETA ** (-jnp.arange(half, dtype=F32) / half)
    sign = jnp.asarray(np.where((np.arange(LANES) % HEAD_DIM) < half, -1.0, 1.0)[None, :], F32)
    lane = np.arange(LANES)
    sel = jnp.asarray(np.stack([lane[:, None] == j * half + lane[None, :] % half for j in range(per_row)]), BF16)
    pos = jnp.repeat(positions.astype(F32).reshape(n // per_row, per_row), half, axis=1)
    tm = min(TOKEN_TILE, n)
    return pl.pallas_call(
        _rope_table_kernel,
        out_shape=[jax.ShapeDtypeStruct((n, LANES), F32)] * 2,
        grid=(n // tm,),
        in_specs=[pl.BlockSpec((tm // per_row, LANES), lambda i: (i, 0)), _const_spec((1, LANES)),
                  _const_spec(sel.shape), _const_spec((1, LANES))],
        out_specs=[pl.BlockSpec((tm, LANES), lambda i: (i, 0))] * 2,
        compiler_params=_params(1),
        name="rope_tables",
    )(pos, jnp.tile(inv, per_row)[None, :], sel, sign)


def _swap_halves(x):
    w = x.shape[-1]
    half = HEAD_DIM // 2
    lane = lax.broadcasted_iota(jnp.int32, x.shape, 1)
    first = (lane % HEAD_DIM) < half
    return jnp.where(first, pltpu.roll(x, w - half, 1), pltpu.roll(x, half, 1))


def _attention_pieces(sink_ref, q_ref, k_ref, v_ref, o_ref, first_tile):
    blocks = q_ref.shape[0] // WINDOW
    group = ATT_HEADS // ATT_KV_HEADS
    outs = [[None] * ATT_HEADS for _ in range(blocks)]
    order = [(hd, r) for r in range(blocks) for hd in range(ATT_HEADS)]

    def piece(hd, r, last):
        kv = hd // group
        lanes = slice(kv * HEAD_DIM, (kv + 1) * HEAD_DIM)
        keys = slice(r * WINDOW, (r + 2) * WINDOW)
        probs = []

        def scores():
            qi = lax.broadcasted_iota(jnp.int32, (WINDOW, 2 * WINDOW), 0)
            sj = lax.broadcasted_iota(jnp.int32, (WINDOW, 2 * WINDOW), 1)
            rel = qi + WINDOW - sj
            mask = (rel >= 0) & (rel < WINDOW)
            if r == 0:
                mask = mask & ((sj >= WINDOW) | jnp.logical_not(first_tile))
            sink = sink_ref[hd]
            qh = q_ref[r * WINDOW:(r + 1) * WINDOW, hd * HEAD_DIM:(hd + 1) * HEAD_DIM]
            s = _dot_nt(qh, k_ref[keys, lanes]) * (HEAD_DIM ** -0.5)
            s = jnp.where(mask, s, MASK_VALUE)
            m = jnp.maximum(jnp.max(s, axis=-1, keepdims=True), sink)
            pr = jnp.where(mask, jnp.exp(s - m), 0.0)
            denom = jnp.sum(pr, axis=-1, keepdims=True) + jnp.exp(sink - m)
            probs.append((pr * (1.0 / denom)).astype(BF16))

        def values():
            outs[r][hd] = _dot(probs[0], v_ref[keys, lanes])
            if last:
                for rr in range(blocks):
                    o_ref[rr * WINDOW:(rr + 1) * WINDOW, :] = jnp.concatenate(outs[rr], axis=1)

        return scores, values

    return [piece(hd, r, (hd, r) == order[-1]) for hd, r in order]


def _sgu_pieces(u_ref, v_ref, lng_ref, w_ref, bias_ref, o_ref):
    def piece(c):
        def run():
            rows = slice(c * SG_CHUNK, (c + 1) * SG_CHUNK)
            u = jax.nn.gelu(u_ref[rows, :])
            v = jax.nn.gelu(v_ref[rows, :])
            mu = jnp.mean(v, axis=-1, keepdims=True)
            vc = v - mu
            v = (vc * lax.rsqrt(jnp.mean(vc * vc, axis=-1, keepdims=True) + EPS) * lng_ref[...]).astype(BF16)
            ti = lax.broadcasted_iota(jnp.int32, (SG_CHUNK, SG_CHUNK), 0)
            si = lax.broadcasted_iota(jnp.int32, (SG_CHUNK, SG_CHUNK), 1)
            lane_group = lax.broadcasted_iota(jnp.int32, (SG_CHUNK, SG_WIDTH), 1) // HEAD_DIM
            mix = bias_ref[...]
            for g in range(SG_GROUPS):
                wg = jnp.where(si <= ti, w_ref[g], 0.0).astype(BF16)
                mix = mix + jnp.where(lane_group == g, _dot(wg, v), 0.0)
            o_ref[rows, :] = u * mix
        return run

    return [piece(c) for c in range(u_ref.shape[0] // SG_CHUNK)]


def _group_cumsum(x):
    r = lax.broadcasted_iota(jnp.int32, x.shape, 0) % HG_SUB
    s = 1
    while s < HG_SUB:
        x = x + jnp.where(r >= s, pltpu.roll(x, s, 0), 0.0)
        s *= 2
    return x


def _group_last(x):
    n = x.shape[0]
    r = lax.broadcasted_iota(jnp.int32, x.shape, 0) % HG_SUB
    t = jnp.where(r == HG_SUB - 1, x, 0.0)
    s = 1
    while s < HG_SUB:
        t = t + pltpu.roll(t, n - s, 0)
        s *= 2
    return t


def _head_block_mask(shape):
    r = lax.broadcasted_iota(jnp.int32, shape, 0) // HEAD_DIM
    c = lax.broadcasted_iota(jnp.int32, shape, 1) // HEAD_DIM
    return r == c


def _dot_tn(a, b):
    return lax.dot_general(a, b, (((0,), (0,)), ((), ())), preferred_element_type=F32)


HG_BLOCK = 128
HG_STEPS = 8


def _hgrn_stages(z_ref, lbl_ref, gn_ref, o_ref, s, layer, n_slots):
    w = HG_WIDTH
    tile = z_ref.shape[0]
    groups = tile // HG_SUB
    half = HG_SUB // 2
    piece = groups * half
    pair = 2 * HEAD_DIM
    pairs = w // pair

    def prepare(blk):
        def run():
            rows = slice(blk * HG_BLOCK, (blk + 1) * HG_BLOCK)
            q = z_ref[rows, 0:w]
            x = z_ref[rows, w:2 * w]
            v = z_ref[rows, 2 * w:3 * w]
            lg = lbl_ref[...]
            e = jnp.exp(lg - jnp.max(lg, axis=0, keepdims=True))
            probs = e / jnp.sum(e, axis=0, keepdims=True)
            lb = jnp.sum(probs[0:layer + 1], axis=0, keepdims=True) - probs[0:1]
            lb_floor = jnp.maximum(lb, LB_FLOOR)
            t = jnp.exp(-jnp.abs(x))
            r = 1.0 / (1.0 + t)
            tr = t * r
            pos = x >= 0.0
            logf = jnp.log(lb_floor + (1.0 - lb) * jnp.where(pos, r, tr))
            k = (1.0 - lb) * jnp.where(pos, tr, r) + (lb - lb_floor)
            b = _group_cumsum(logf)
            b_last = _group_last(b)
            s["qd"][rows, :] = (q * jnp.exp(b)).astype(BF16)
            s["kd"][rows, :] = (k * jnp.exp(b_last - b)).astype(BF16)
            s["vb"][rows, :] = v.astype(BF16)
            s["k"][rows, :] = k
            s["b2"][rows, :] = b * LOG2E
            per = HG_BLOCK // HG_SUB
            s["decay"][blk * per:(blk + 1) * per, :] = jnp.exp(b_last.reshape(per, HG_SUB, w)[:, 0, :])
            half_rows = slice(blk * HG_BLOCK // 2, (blk + 1) * HG_BLOCK // 2)
            s["o_lo"][half_rows, :] = jnp.zeros((HG_BLOCK // 2, w), F32)
            s["o_hi"][half_rows, :] = jnp.zeros((HG_BLOCK // 2, w), F32)
        return run

    def increments(gk):
        def run():
            blockdiag = _head_block_mask((pair, pair))
            for i in range(HG_STEPS):
                rows = slice((gk * HG_STEPS + i) * HG_SUB, (gk * HG_STEPS + i + 1) * HG_SUB)
                for p in range(pairs):
                    lanes = slice(p * pair, (p + 1) * pair)
                    upd = _dot_tn(s["vb"][rows, lanes], s["kd"][rows, lanes])
                    s["upd"][gk % 2, i, p] = jnp.where(blockdiag, upd, 0.0)
        return run

    def recur(gk):
        def run():
            st = [s["st"][p] for p in range(pairs)]
            for i in range(HG_STEPS):
                m = gk * HG_STEPS + i
                decay = s["decay"][m:m + 1, :]
                for p in range(pairs):
                    s["seen"][m, p] = st[p].astype(BF16)
                    st[p] = st[p] * decay[:, p * pair:(p + 1) * pair] + s["upd"][gk % 2, i, p]
            for p in range(pairs):
                s["st"][p] = st[p]
        return run

    def cross(gk):
        def run():
            for i in range(HG_STEPS):
                m = gk * HG_STEPS + i
                rows = slice(m * HG_SUB, (m + 1) * HG_SUB)
                for p in range(pairs):
                    lanes = slice(p * pair, (p + 1) * pair)
                    s["raw"][rows, lanes] = _dot_nt(s["qd"][rows, lanes], s["seen"][m, p])
        return run

    grp = lambda a: a.reshape(groups, HG_SUB, w)

    def key_products(j):
        def run():
            b3 = grp(s["b2"][...])
            q3 = grp(z_ref[:, 0:w])
            bj = b3[:, j:j + 1, :]
            kj = grp(s["k"][...])[:, j:j + 1, :]
            row = lax.broadcasted_iota(jnp.int32, (groups, half, w), 1)
            if j < half:
                dec = jnp.exp2(jnp.where(row >= j, b3[:, :half] - bj, MASK_VALUE))
                s["t"][j % 2, 0] = (q3[:, :half] * kj * dec).reshape(piece, w).astype(BF16)
                dec = jnp.exp2(b3[:, half:] - bj)
            else:
                dec = jnp.exp2(jnp.where(row >= j - half, b3[:, half:] - bj, MASK_VALUE))
            s["t"][j % 2, 1] = (q3[:, half:] * kj * dec).reshape(piece, w).astype(BF16)
        return run

    def key_sums(j):
        def run():
            vj = grp(z_ref[:, 2 * w:3 * w])[:, j:j + 1, :]
            ones_bd = jnp.where(_head_block_mask((w, w)), 1.0, 0.0).astype(BF16)
            head_sums = lambda t: _dot(t, ones_bd).reshape(groups, half, w)
            if j < half:
                s["o_lo"][...] += (head_sums(s["t"][j % 2, 0]) * vj).reshape(piece, w)
            s["o_hi"][...] += (head_sums(s["t"][j % 2, 1]) * vj).reshape(piece, w)
        return run

    def finish():
        ones_bd = jnp.where(_head_block_mask((w, w)), 1.0, 0.0).astype(BF16)
        intra = jnp.concatenate([s["o_lo"][...].reshape(groups, half, w), s["o_hi"][...].reshape(groups, half, w)],
                                axis=1).reshape(tile, w)
        o = s["raw"][...] + intra
        sq = o * o
        hi = sq.astype(BF16)
        lo = (sq - hi.astype(F32)).astype(BF16)
        ms = (_dot(hi, ones_bd) + _dot(lo, ones_bd)) * (1.0 / HEAD_DIM)
        gate = jax.nn.sigmoid(z_ref[:, 3 * w:4 * w])
        o_ref[...] = o * lax.rsqrt(ms + EPS) * gn_ref[...] * gate

    n_groups = groups // HG_STEPS
    assert tile // HG_BLOCK == n_groups and n_groups < n_slots
    slots = [[] for _ in range(n_slots)]
    for t in range(n_groups + 3):
        slot = slots[min(t, n_slots - 1)]
        if t < n_groups:
            slot.append(prepare(t))
        if 0 <= t - 1 < n_groups:
            slot.append(increments(t - 1))
        if 0 <= t - 2 < n_groups:
            slot.append(recur(t - 2))
        if 0 <= t - 3 < n_groups:
            slot.append(cross(t - 3))
    keys = [key_products(0)]
    for j in range(1, HG_SUB):
        keys += [key_products(j), key_sums(j - 1)]
    keys.append(key_sums(HG_SUB - 1))
    key_slots = n_slots - n_groups
    for n, stage in enumerate(keys):
        slots[n_groups + n * key_slots // len(keys)].append(stage)
    return slots, finish


def _hgrn_scratch(tile):
    w = HG_WIDTH
    pair = 2 * HEAD_DIM
    steps = tile // HG_SUB
    return [("st", pltpu.VMEM((w // pair, pair, pair), F32)),
            ("qd", pltpu.VMEM((tile, w), BF16)),
            ("kd", pltpu.VMEM((tile, w), BF16)),
            ("vb", pltpu.VMEM((tile, w), BF16)),
            ("k", pltpu.VMEM((tile, w), F32)),
            ("b2", pltpu.VMEM((tile, w), F32)),
            ("decay", pltpu.VMEM((steps, w), F32)),
            ("upd", pltpu.VMEM((2, HG_STEPS, w // pair, pair, pair), F32)),
            ("seen", pltpu.VMEM((steps, w // pair, pair, pair), BF16)),
            ("t", pltpu.VMEM((2, 2, tile // 2, w), BF16)),
            ("raw", pltpu.VMEM((tile, w), F32)),
            ("o_lo", pltpu.VMEM((tile // 2, w), F32)),
            ("o_hi", pltpu.VMEM((tile // 2, w), F32))]


def kernel(x, p, positions, norm_gains, w_in, w_out, ffn1_gate_up, ffn1_down, ffn2_gate_up, ffn2_down,
           hgrn_lb_logits, hgrn_norm_gain, attn_sinks, sg_ln_gain, sg_spatial_w, sg_spatial_b, ple_proj, ple_gate):
    batch, seq, _ = x.shape
    depth = norm_gains.shape[0]
    n = batch * seq
    h = x.reshape(n, D_MODEL)
    cos, sin = _rope_tables(positions)

    bf = lambda w: w.astype(BF16)
    wgu1, wd1, win, wgu2, wd2 = bf(ffn1_gate_up), bf(ffn1_down), bf(w_in), bf(ffn2_gate_up), bf(ffn2_down)
    wout, wgate, wproj = bf(w_out), bf(ple_gate), bf(ple_proj)
    for l in range(depth):
        bias_tab = jnp.repeat(sg_spatial_b[l].T, HEAD_DIM, axis=1)
        h, zh, o_b, o_c = _premix(h, norm_gains, wgu1, wd1, win, attn_sinks[l], cos, sin, sg_ln_gain[l][None, :],
                                  sg_spatial_w[l], bias_tab, l, seq)
        h = _postmix(h, zh, o_b, o_c, p.reshape(depth, n, PLE_DIM), hgrn_lb_logits,
                     jnp.tile(hgrn_norm_gain[l], HG_HEADS)[None, :], norm_gains, wout, wgu2, wd2, wgate, wproj, l, seq)
    return h.reshape(batch, seq, D_MODEL)
```
